```python
import jax
import jax.numpy as jnp
from jax import lax
import numpy as np

D_MODEL = 2048
BATCH = 8
SEQ = 4096
DEPTH = 4
DEC_BATCH = 32
DEC_SEQ = 64
PAST_LEN = 1024

CHUNK = 64
N_BRANCH = 3
BRANCH_W = 1024
POOL_WINDOWS = (2, 4, 8, 16)
N_POOL_GROUPS = 4
POOL_GROUP_W = BRANCH_W // N_POOL_GROUPS
POOL_STATE = 15
SB_HEADS = 8
SB_HEAD_DIM = BRANCH_W // SB_HEADS
SB_Q_BLOCK = 128
ML_HEADS = 4
ML_HEAD_DIM = BRANCH_W // ML_HEADS
F_BIAS_INIT = 3.0
RMS_EPS = 1e-6
NEG_BIG = -1e30

OFF_POOL = 0
OFF_SB_Q = OFF_POOL + BRANCH_W
OFF_SB_K = OFF_SB_Q + BRANCH_W
OFF_SB_V = OFF_SB_K + BRANCH_W
OFF_ML_Q = OFF_SB_V + BRANCH_W
OFF_ML_K = OFF_ML_Q + BRANCH_W
OFF_ML_V = OFF_ML_K + BRANCH_W
OFF_ML_O = OFF_ML_V + BRANCH_W
OFF_ML_I = OFF_ML_O + BRANCH_W
OFF_ML_F = OFF_ML_I + ML_HEADS
OFF_Z = OFF_ML_F + ML_HEADS
OFF_GATE = OFF_Z + N_BRANCH * BRANCH_W
IN_COLS = OFF_GATE + N_BRANCH * D_MODEL

kernel_name = "hybrid_pool_stickbreak_mlstm_stream_step"


def rmsnorm(x, g):
    xf = x.astype(jnp.float32)
    xf = xf * lax.rsqrt(jnp.mean(xf * xf, axis=-1, keepdims=True) + RMS_EPS)
    return (xf * g.astype(jnp.float32)).astype(x.dtype)


def pool_mixer(u, prev, pos0, w_pool, scale):
    b, t, c = u.shape
    full = jnp.concatenate([prev.astype(u.dtype), u], axis=1)
    cs = jnp.cumsum(jnp.concatenate([jnp.zeros((b, 1, c), jnp.float32), full.astype(jnp.float32)], axis=1), axis=1)
    pos = pos0 + jnp.arange(t, dtype=jnp.int32)
    base = POOL_STATE + 1
    outs = []
    for gi, w in enumerate(POOL_WINDOWS):
        lo, hi = gi * POOL_GROUP_W, (gi + 1) * POOL_GROUP_W
        wsum = cs[:, base:base + t, lo:hi] - cs[:, base - w:base - w + t, lo:hi]
        cnt = jnp.minimum(pos + 1, w).astype(jnp.float32)[None, :, None]
        pooled = wsum / cnt - u[:, :, lo:hi].astype(jnp.float32)
        outs.append(jnp.einsum('btc,cd->btd', pooled, w_pool[gi].astype(jnp.float32)))
    y = jnp.concatenate(outs, axis=-1) * scale.astype(jnp.float32)
    return y.astype(u.dtype), full[:, -POOL_STATE:]


def stick_breaking(q, k, v, q_pos0):
    b, tq, h, dh = q.shape
    scale = dh ** -0.5
    outs = []
    for start in range(0, tq, SB_Q_BLOCK):
        end = min(start + SB_Q_BLOCK, tq)
        kend = q_pos0 + end
        qb = q[:, start:end].astype(jnp.float32)
        kb = k[:, :kend].astype(jnp.float32)
        vb = v[:, :kend].astype(jnp.float32)
        z = jnp.einsum('bqhd,bkhd->bhqk', qb, kb) * scale
        qpos = q_pos0 + start + jnp.arange(end - start)
        kpos = jnp.arange(kend)
        valid = kpos[None, :] < qpos[:, None]
        log_1m = jnp.where(valid, jax.nn.log_sigmoid(-z), 0.0)
        after = lax.cumsum(log_1m, axis=3, reverse=True) - log_1m
        a = jnp.where(valid, jnp.exp(jax.nn.log_sigmoid(z) + after), 0.0)
        outs.append(jnp.einsum('bhqk,bkhd->bqhd', a, vb))
    return jnp.concatenate(outs, axis=1).astype(q.dtype)


def mlstm_chunkwise(q, k, v, i_pre, log_f, c0, n0, m0):
    b, t, h, dh = q.shape
    nc = -(-t // CHUNK)
    pad = nc * CHUNK - t
    f32 = jnp.float32

    def to_chunks(a, fill):
        a = a.astype(f32)
        a = jnp.pad(a, ((0, 0), (0, pad)) + ((0, 0),) * (a.ndim - 2), constant_values=fill)
        return a.reshape((b, nc, CHUNK) + a.shape[2:])

    qs = jnp.transpose(to_chunks(q, 0.0), (1, 0, 3, 2, 4))
    ks = jnp.transpose(to_chunks(k * (dh ** -0.5), 0.0), (1, 0, 3, 2, 4))
    vs = jnp.transpose(to_chunks(v, 0.0), (1, 0, 3, 2, 4))
    is_ = jnp.transpose(to_chunks(i_pre, NEG_BIG), (1, 0, 3, 2))
    fs = jnp.transpose(to_chunks(log_f, 0.0), (1, 0, 3, 2))
    causal = jnp.tril(jnp.ones((CHUNK, CHUNK), dtype=bool))

    def step(carry, inp):
        c, n, m = carry
        qc, kc, vc, ic, fc = inp
        bcum = jnp.cumsum(fc, axis=-1)
        d = bcum[..., :, None] - bcum[..., None, :] + ic[..., None, :]
        d = jnp.where(causal, d, -jnp.inf)
        g = bcum + m[..., None]
        m_t = jnp.maximum(g, jnp.max(d, axis=-1))
        w_intra = jnp.exp(d - m_t[..., None])
        w_inter = jnp.exp(g - m_t)
        s = jnp.einsum('bhld,bhsd->bhls', qc, kc) * w_intra
        num = jnp.einsum('bhls,bhsv->bhlv', s, vc) + w_inter[..., None] * jnp.einsum('bhlk,bhkv->bhlv', qc, c)
        den = jnp.sum(s, axis=-1) + w_inter * jnp.einsum('bhlk,bhk->bhl', qc, n)
        h_out = num / jnp.maximum(jnp.abs(den), jnp.exp(-m_t))[..., None]
        b_last = bcum[..., -1]
        m_new = m_t[..., -1]
        decay = jnp.exp(b_last + m - m_new)
        w_end = jnp.exp(b_last[..., None] - bcum + ic - m_new[..., None])
        c_new = decay[..., None, None] * c + jnp.einsum('bhs,bhsk,bhsv->bhkv', w_end, kc, vc)
        n_new = decay[..., None] * n + jnp.einsum('bhs,bhsk->bhk', w_end, kc)
        return (c_new, n_new, m_new), h_out

    (c_f, n_f, m_f), hs = lax.scan(step, (c0.astype(f32), n0.astype(f32), m0.astype(f32)), (qs, ks, vs, is_, fs))
    hs = jnp.transpose(hs, (1, 0, 3, 2, 4)).reshape(b, nc * CHUNK, h, dh)[:, :t]
    return hs, c_f, n_f, m_f


def trunk_layer(x, pos0, pool_prev, sb_k_past, sb_v_past, ml_c, ml_n, ml_m,
                g_pre, w_in, b_if, w_pool, pool_scale, ml_hnorm, w_branch, w_out, g_post):
    b, t, _ = x.shape
    f32 = jnp.float32
    xn = rmsnorm(x, g_pre)

    def cols(off, width):
        return jnp.einsum('btd,dc->btc', xn, w_in[:, off:off + width])

    pool_y, pool_new = pool_mixer(cols(OFF_POOL, BRANCH_W), pool_prev, pos0, w_pool, pool_scale)

    q_sb = cols(OFF_SB_Q, BRANCH_W).reshape(b, t, SB_HEADS, SB_HEAD_DIM)
    k_sb = cols(OFF_SB_K, BRANCH_W).reshape(b, t, SB_HEADS, SB_HEAD_DIM)
    v_sb = cols(OFF_SB_V, BRANCH_W).reshape(b, t, SB_HEADS, SB_HEAD_DIM)
    if sb_k_past is None:
        k_all, v_all = k_sb, v_sb
    else:
        k_all = jnp.concatenate([sb_k_past.astype(k_sb.dtype), k_sb], axis=1)
        v_all = jnp.concatenate([sb_v_past.astype(v_sb.dtype), v_sb], axis=1)
    sb_y = stick_breaking(q_sb, k_all, v_all, pos0).reshape(b, t, BRANCH_W)

    q_ml = cols(OFF_ML_Q, BRANCH_W).reshape(b, t, ML_HEADS, ML_HEAD_DIM)
    k_ml = cols(OFF_ML_K, BRANCH_W).reshape(b, t, ML_HEADS, ML_HEAD_DIM)
    v_ml = cols(OFF_ML_V, BRANCH_W).reshape(b, t, ML_HEADS, ML_HEAD_DIM)
    i_pre = cols(OFF_ML_I, ML_HEADS).astype(f32) + b_if[0].astype(f32)
    log_f = jax.nn.log_sigmoid(cols(OFF_ML_F, ML_HEADS).astype(f32) + b_if[1].astype(f32))
    h_ml, c_new, n_new, m_new = mlstm_chunkwise(q_ml, k_ml, v_ml, i_pre, log_f, ml_c, ml_n, ml_m)
    h_ml = h_ml * lax.rsqrt(jnp.mean(h_ml * h_ml, axis=-1, keepdims=True) + RMS_EPS)
    h_ml = h_ml * ml_hnorm.astype(f32).reshape(ML_HEADS, ML_HEAD_DIM)
    ml_y = (h_ml.reshape(b, t, BRANCH_W) * jax.nn.sigmoid(cols(OFF_ML_O, BRANCH_W).astype(f32))).astype(x.dtype)

    merged = None
    for bi, yb in enumerate((pool_y, sb_y, ml_y)):
        gated = yb * jax.nn.silu(cols(OFF_Z + bi * BRANCH_W, BRANCH_W))
        term = jax.nn.sigmoid(cols(OFF_GATE + bi * D_MODEL, D_MODEL)) * jnp.einsum('btw,wd->btd', gated, w_branch[bi])
        merged = term if merged is None else merged + term
    y = jnp.einsum('btd,de->bte', merged, w_out).astype(x.dtype)
    x_out = x + rmsnorm(y, g_post)
    return x_out, pool_new, k_sb, v_sb, c_new, n_new, m_new


def setup_inputs(seed: int = 0) -> dict:
    key = jax.random.key(seed)
    ks = jax.random.split(key, 20)
    nrm = jax.random.normal
    f32 = jnp.float32
    x_prompt = nrm(ks[0], (BATCH, SEQ, D_MODEL), f32)
    x_sample = nrm(ks[1], (DEC_BATCH, DEC_SEQ, D_MODEL), f32)
    cache_sb_k = nrm(ks[2], (DEPTH, DEC_BATCH, PAST_LEN, SB_HEADS, SB_HEAD_DIM), f32)
    cache_sb_v = nrm(ks[3], (DEPTH, DEC_BATCH, PAST_LEN, SB_HEADS, SB_HEAD_DIM), f32)
    state_pool = nrm(ks[4], (DEPTH, DEC_BATCH, POOL_STATE, BRANCH_W), f32)
    state_ml_c = 0.1 * nrm(ks[5], (DEPTH, DEC_BATCH, ML_HEADS, ML_HEAD_DIM, ML_HEAD_DIM), f32)
    state_ml_n = 0.1 * nrm(ks[6], (DEPTH, DEC_BATCH, ML_HEADS, ML_HEAD_DIM), f32)
    state_ml_m = 0.5 * nrm(ks[7], (DEPTH, DEC_BATCH, ML_HEADS), f32)
    g_pre = 1.0 + 0.05 * nrm(ks[8], (DEPTH, D_MODEL), f32)
    w_in = nrm(ks[9], (DEPTH, D_MODEL, IN_COLS), f32) * (D_MODEL ** -0.5)
    b_if = jnp.stack([0.1 * nrm(ks[10], (DEPTH, ML_HEADS), f32),
                      F_BIAS_INIT + 0.1 * nrm(ks[11], (DEPTH, ML_HEADS), f32)], axis=1)
    w_pool = nrm(ks[12], (DEPTH, N_POOL_GROUPS, POOL_GROUP_W, POOL_GROUP_W), f32) * (POOL_GROUP_W ** -0.5)
    pool_scale = 1.0 + 0.1 * nrm(ks[13], (DEPTH, BRANCH_W), f32)
    ml_hnorm = 1.0 + 0.05 * nrm(ks[14], (DEPTH, BRANCH_W), f32)
    w_branch = nrm(ks[15], (DEPTH, N_BRANCH, BRANCH_W, D_MODEL), f32) * (BRANCH_W ** -0.5)
    w_out = nrm(ks[16], (DEPTH, D_MODEL, D_MODEL), f32) * (D_MODEL ** -0.5)
    g_post = 1.0 + 0.05 * nrm(ks[17], (DEPTH, D_MODEL), f32)
    return {"x_prompt": x_prompt, "x_sample": x_sample,
            "cache_sb_k": cache_sb_k, "cache_sb_v": cache_sb_v, "state_pool": state_pool,
            "state_ml_c": state_ml_c, "state_ml_n": state_ml_n, "state_ml_m": state_ml_m,
            "g_pre": g_pre, "w_in": w_in, "b_if": b_if, "w_pool": w_pool, "pool_scale": pool_scale,
            "ml_hnorm": ml_hnorm, "w_branch": w_branch, "w_out": w_out, "g_post": g_post}


def reference(x_prompt, x_sample, cache_sb_k, cache_sb_v, state_pool, state_ml_c, state_ml_n, state_ml_m,
              g_pre, w_in, b_if, w_pool, pool_scale, ml_hnorm, w_branch, w_out, g_post):
    f32 = jnp.float32
    bp = x_prompt.shape[0]
    past_len = cache_sb_k.shape[2]
    xp, xs = x_prompt, x_sample
    pr = [[] for _ in range(6)]
    sa = [[] for _ in range(6)]
    for l in range(DEPTH):
        w = (g_pre[l], w_in[l], b_if[l], w_pool[l], pool_scale[l], ml_hnorm[l], w_branch[l], w_out[l], g_post[l])
        xp, *sp = trunk_layer(xp, 0, jnp.zeros((bp, POOL_STATE, BRANCH_W), xp.dtype), None, None,
                              jnp.zeros((bp, ML_HEADS, ML_HEAD_DIM, ML_HEAD_DIM), f32),
                              jnp.zeros((bp, ML_HEADS, ML_HEAD_DIM), f32),
                              jnp.zeros((bp, ML_HEADS), f32), *w)
        xs, *ss = trunk_layer(xs, past_len, state_pool[l], cache_sb_k[l], cache_sb_v[l],
                              state_ml_c[l], state_ml_n[l], state_ml_m[l], *w)
        for lst, a in zip(pr, sp):
            lst.append(a)
        for lst, a in zip(sa, ss):
            lst.append(a)
    p_pool, p_k, p_v, p_c, p_n, p_m = [jnp.stack(a, axis=0) for a in pr]
    s_pool, s_k, s_v, s_c, s_n, s_m = [jnp.stack(a, axis=0) for a in sa]
    return (xp, xs, p_k, p_v, p_pool, p_c, p_n, p_m, s_k, s_v, s_pool, s_c, s_n, s_m)
```

```python
import functools

import jax
import jax.numpy as jnp
from jax import lax
from jax.experimental import pallas as pl
from jax.experimental.pallas import tpu as pltpu

F32 = jnp.float32
BF16 = jnp.bfloat16

D_MODEL = 2048
DEPTH = 4
BRANCH_W = 1024
N_BRANCH = 3
POOL_WINDOWS = (2, 4, 8, 16)
POOL_GROUP_W = BRANCH_W // len(POOL_WINDOWS)
POOL_STATE = 15
POOL_HALO = 16
SB_HEADS = 8
SB_HEAD_DIM = BRANCH_W // SB_HEADS
ML_HEADS = 4
ML_HEAD_DIM = BRANCH_W // ML_HEADS
RMS_EPS = 1e-6
LANES = 128

OFF_POOL, OFF_SB_Q, OFF_SB_K, OFF_SB_V = 0, 1024, 2048, 3072
OFF_ML_Q, OFF_ML_K, OFF_ML_V, OFF_ML_O = 4096, 5120, 6144, 7168
OFF_ML_I = 8192
OFF_Z = OFF_ML_I + 2 * ML_HEADS
OFF_GATE = OFF_Z + N_BRANCH * BRANCH_W
IN_COLS = OFF_GATE + N_BRANCH * D_MODEL

WB_POOL, WB_SB_K, WB_SB_V, WB_REST = 0, 1024, 2048, 3072
WB_COLS = IN_COLS - 2 * ML_HEADS
REST_COLS = WB_COLS - WB_REST
R_SB_Q, R_ML_Q, R_ML_K, R_ML_V, R_ML_O, R_Z0 = 0, 1, 2, 3, 4, 5
R_GATE0 = 8

VMEM_LIMIT = 56 * 1024 * 1024


def _cparams(*sem):
    return pltpu.CompilerParams(dimension_semantics=sem, vmem_limit_bytes=VMEM_LIMIT)


def _sigmoid(x):
    return 1.0 / (1.0 + jnp.exp(-x))


def _silu(x):
    return x * _sigmoid(x)


def _rmsnorm_kernel(x_ref, g_ref, o_ref):
    x = x_ref[...]
    ms = jnp.mean(x * x, axis=-1, keepdims=True)
    o_ref[...] = (x * lax.rsqrt(ms + RMS_EPS) * g_ref[...]).astype(o_ref.dtype)


def _rmsnorm(x, g_all, layer, tm=512):
    n, d = x.shape
    tm = min(tm, n)
    return pl.pallas_call(
        _rmsnorm_kernel,
        grid=(n // tm,),
        in_specs=[pl.BlockSpec((tm, d), lambda i: (i, 0)),
                  pl.BlockSpec((None, 1, d), lambda i: (layer, 0, 0))],
        out_specs=pl.BlockSpec((tm, d), lambda i: (i, 0)),
        out_shape=jax.ShapeDtypeStruct((n, d), BF16),
        compiler_params=_cparams("parallel"),
        name="rmsnorm",
    )(x, g_all)


def _mm_kernel(a_ref, b_ref, *o_refs):
    acc = jnp.dot(a_ref[...], b_ref[...], preferred_element_type=F32)
    for o_ref in o_refs:
        o_ref[...] = acc.astype(o_ref.dtype)


def _mm(a, w_all, layer, col0, ncols, out_dtypes, tm=1024, tn=1024):
    m, k = a.shape
    tm, tn = min(tm, m), min(tn, ncols)
    assert m % tm == 0 and ncols % tn == 0 and col0 % tn == 0
    cb = col0 // tn
    outs = [jax.ShapeDtypeStruct((m, ncols), dt) for dt in out_dtypes]
    res = pl.pallas_call(
        _mm_kernel,
        grid=(m // tm, ncols // tn),
        in_specs=[pl.BlockSpec((tm, k), lambda i, j: (i, 0)),
                  pl.BlockSpec((None, k, tn), lambda i, j: (layer, 0, j + cb))],
        out_specs=[pl.BlockSpec((tm, tn), lambda i, j: (i, j)) for _ in outs],
        out_shape=outs,
        compiler_params=_cparams("parallel", "arbitrary"),
        name="in_proj",
    )(a, w_all)
    return res


def _if_kernel(a_ref, w_ref, b_ref, o_ref):
    acc = jnp.dot(a_ref[...], w_ref[...], preferred_element_type=F32) + b_ref[...]
    col = lax.broadcasted_iota(jnp.int32, acc.shape, 1)
    log_sig = jnp.minimum(acc, 0.0) - jnp.log(1.0 + jnp.exp(-jnp.abs(acc)))
    o_ref[...] = jnp.where(col >= ML_HEADS, log_sig, acc)


def _if_proj(a, w_if, b_if_pad, layer, tm=1024):
    m, k = a.shape
    tm = min(tm, m)
    return pl.pallas_call(
        _if_kernel,
        grid=(m // tm,),
        in_specs=[pl.BlockSpec((tm, k), lambda i: (i, 0)),
                  pl.BlockSpec((None, k, LANES), lambda i: (layer, 0, 0)),
                  pl.BlockSpec((None, 1, LANES), lambda i: (layer, 0, 0))],
        out_specs=pl.BlockSpec((tm, LANES), lambda i: (i, 0)),
        out_shape=jax.ShapeDtypeStruct((m, LANES), F32),
        compiler_params=_cparams("parallel"),
        name="if_proj",
    )(a, w_if, b_if_pad)


def _pool_kernel(u_ref, halo_ref, prev_ref, z_ref, w_ref, s_ref, o_ref, full_ref, *, tt, pos0):
    t = pl.program_id(1)
    full_ref[0:POOL_HALO, :] = jnp.where(t == 0, prev_ref[0], halo_ref[0])
    u = u_ref[0]
    full_ref[POOL_HALO:POOL_HALO + tt, :] = u
    ys = []
    for gi, w in enumerate(POOL_WINDOWS):
        lo = gi * POOL_GROUP_W
        wsum = u[:, lo:lo + POOL_GROUP_W]
        for j in range(1, w):
            wsum = wsum + full_ref[POOL_HALO - j:POOL_HALO - j + tt, lo:lo + POOL_GROUP_W]
        if pos0 + 1 >= w:
            mean = wsum * (1.0 / w)
        else:
            pos = pos0 + t * tt + lax.broadcasted_iota(jnp.int32, (tt, 1), 0)
            mean = wsum / jnp.minimum(pos + 1, w).astype(F32)
        pooled = mean - u[:, lo:lo + POOL_GROUP_W]
        ys.append(jnp.dot(pooled.astype(BF16), w_ref[gi], preferred_element_type=F32))
    y = jnp.concatenate(ys, axis=-1) * s_ref[...]
    o_ref[0] = (y * _silu(z_ref[0].astype(F32))).astype(o_ref.dtype)


def _pool(u, prev16, rest, w_pool_all, scale_all, layer, pos0, tt=512):
    b, t, w = u.shape
    tt = min(tt, t)
    assert t % tt == 0 and tt % POOL_HALO == 0
    hb = tt // POOL_HALO
    kern = functools.partial(_pool_kernel, tt=tt, pos0=pos0)
    return pl.pallas_call(
        kern,
        grid=(b, t // tt),
        in_specs=[pl.BlockSpec((1, tt, w), lambda i, j: (i, j, 0)),
                  pl.BlockSpec((1, POOL_HALO, w), lambda i, j: (i, jnp.maximum(j * hb - 1, 0), 0)),
                  pl.BlockSpec((1, POOL_HALO, w), lambda i, j: (i, 0, 0)),
                  pl.BlockSpec((1, tt, w), lambda i, j: (i, j, R_Z0)),
                  pl.BlockSpec((None, len(POOL_WINDOWS), POOL_GROUP_W, POOL_GROUP_W), lambda i, j: (layer, 0, 0, 0)),
                  pl.BlockSpec((None, 1, w), lambda i, j: (layer, 0, 0))],
        out_specs=pl.BlockSpec((1, tt, w), lambda i, j: (i, j, 0)),
        out_shape=jax.ShapeDtypeStruct((b, t, w), BF16),
        scratch_shapes=[pltpu.VMEM((POOL_HALO + tt, w), F32)],
        compiler_params=_cparams("parallel", "arbitrary"),
        name="pool",
    )(u, u, prev16, rest, w_pool_all, scale_all)


def _sb_kernel(q_ref, k_ref, v_ref, z_ref, tri_ref, o_ref, *, tq, tk, q_pos0, n_kblocks):
    i = pl.program_id(2)
    q = q_ref[0]
    q_start = q_pos0 + i * tq
    scale = SB_HEAD_DIM ** -0.5
    tri = tri_ref[...]

    def block(kb, carry, acc, masked):
        k0 = pl.multiple_of(kb * tk, tk)
        kblk = k_ref[0, pl.ds(k0, tk), :].astype(BF16)
        vblk = v_ref[0, pl.ds(k0, tk), :].astype(BF16)
        z = lax.dot_general(q, kblk, (((1,), (1,)), ((), ())), preferred_element_type=F32) * scale
        softplus = jnp.maximum(z, 0.0) + jnp.log(1.0 + jnp.exp(-jnp.abs(z)))
        log_1m = -softplus
        if masked:
            qpos = q_start + lax.broadcasted_iota(jnp.int32, (tq, tk), 0)
            kpos = k0 + lax.broadcasted_iota(jnp.int32, (tq, tk), 1)
            valid = kpos < qpos
            log_1m = jnp.where(valid, log_1m, 0.0)
        hi = log_1m.astype(BF16)
        lo = (log_1m - hi.astype(F32)).astype(BF16)
        after = (jnp.dot(hi, tri, preferred_element_type=F32)
                 + jnp.dot(lo, tri, preferred_element_type=F32) + carry)
        a = jnp.exp((z - softplus) + after)
        if masked:
            a = jnp.where(valid, a, 0.0)
        acc = acc + jnp.dot(a.astype(BF16), vblk, preferred_element_type=F32)
        carry = carry + jnp.sum(log_1m, axis=1, keepdims=True)
        return carry, acc

    n_full = q_start // tk
    n_used = jnp.minimum((q_start + tq + tk - 1) // tk, n_kblocks)
    carry = jnp.zeros((tq, 1), F32)
    acc = jnp.zeros((tq, SB_HEAD_DIM), F32)
    carry, acc = lax.fori_loop(
        0, n_used - n_full, lambda s, c: block(n_used - 1 - s, c[0], c[1], True), (carry, acc))
    carry, acc = lax.fori_loop(
        0, n_full, lambda s, c: block(n_full - 1 - s, c[0], c[1], False), (carry, acc))
    o_ref[0] = (acc * _silu(z_ref[0].astype(F32))).astype(o_ref.dtype)


def _stick_breaking(rest, k_all, v_all, tri, q_pos0, tq=256):
    b, t_q, _ = rest.shape
    t_k = k_all.shape[1]
    tk = tri.shape[0]
    tq = min(tq, t_q)
    assert t_q % tq == 0 and t_k % tk == 0 and t_k >= q_pos0 + t_q
    hq = R_SB_Q * SB_HEADS
    hz = (R_Z0 + 1) * SB_HEADS
    kern = functools.partial(_sb_kernel, tq=tq, tk=tk, q_pos0=q_pos0, n_kblocks=t_k // tk)
    return pl.pallas_call(
        kern,
        grid=(b, SB_HEADS, t_q // tq),
        in_specs=[pl.BlockSpec((1, tq, SB_HEAD_DIM), lambda bi, h, i: (bi, i, hq + h)),
                  pl.BlockSpec((1, t_k, SB_HEAD_DIM), lambda bi, h, i: (bi, 0, h)),
                  pl.BlockSpec((1, t_k, SB_HEAD_DIM), lambda bi, h, i: (bi, 0, h)),
                  pl.BlockSpec((1, tq, SB_HEAD_DIM), lambda bi, h, i: (bi, i, hz + h)),
                  pl.BlockSpec((tk, tk), lambda bi, h, i: (0, 0))],
        out_specs=pl.BlockSpec((1, tq, SB_HEAD_DIM), lambda bi, h, i: (bi, i, h)),
        out_shape=jax.ShapeDtypeStruct((b, t_q, BRANCH_W), BF16),
        compiler_params=_cparams("parallel", "parallel", "arbitrary"),
        name="stick_breaking",
    )(rest, k_all, v_all, rest, tri)


def _mlstm_kernel(q_ref, k_ref, v_ref, og_ref, z_ref, ifc_ref, ifr_ref, hn_ref, c0_ref, n0_ref, m0_ref,
                  y_ref, c_ref, n_ref, m_ref, *, chunk):
    ci = pl.program_id(1)

    @pl.when(ci == 0)
    def _():
        c_ref[...] = c0_ref[...]
        n_ref[...] = n0_ref[...]
        m_ref[...] = m0_ref[...]

    row = lax.broadcasted_iota(jnp.int32, (chunk, chunk), 0)
    col = lax.broadcasted_iota(jnp.int32, (chunk, chunk), 1)
    causal = col <= row
    k_scale = ML_HEAD_DIM ** -0.5
    for h in range(ML_HEADS):
        sl = slice(h * ML_HEAD_DIM, (h + 1) * ML_HEAD_DIM)
        qh, kh, vh = q_ref[0, :, sl], k_ref[0, :, sl], v_ref[0, :, sl]
        i_col = ifc_ref[0, :, h:h + 1]
        f_col = ifc_ref[0, :, ML_HEADS + h:ML_HEADS + h + 1]
        i_row = ifr_ref[0, 0, h:h + 1, :]
        f_row = ifr_ref[0, 0, ML_HEADS + h:ML_HEADS + h + 1, :]
        m_prev = m_ref[0, h, :, 0:1]
        n_prev = n_ref[0, h]
        c_prev = c_ref[0, h]

        bcum_col = jnp.sum(jnp.where(causal, f_row, 0.0), axis=1, keepdims=True)
        bcum_row = jnp.sum(jnp.where(row <= col, f_col, 0.0), axis=0, keepdims=True)
        d = jnp.where(causal, bcum_col - bcum_row + i_row, -jnp.inf)
        g = bcum_col + m_prev
        m_t = jnp.maximum(g, jnp.max(d, axis=1, keepdims=True))
        w_intra = jnp.exp(d - m_t)
        w_inter = jnp.exp(g - m_t)
        s = lax.dot_general(qh, kh, (((1,), (1,)), ((), ())), preferred_element_type=F32) * k_scale * w_intra
        q_c = jnp.dot(qh, c_prev.astype(BF16), preferred_element_type=F32)
        num = jnp.dot(s.astype(BF16), vh, preferred_element_type=F32) + w_inter * q_c
        q_n = jnp.sum(qh.astype(F32) * n_prev, axis=1, keepdims=True)
        den = jnp.sum(s, axis=1, keepdims=True) + w_inter * q_n
        h_out = num / jnp.maximum(jnp.abs(den), jnp.exp(-m_t))

        b_last = jnp.sum(f_row, axis=1, keepdims=True)
        m_new = jnp.maximum(b_last + m_prev, jnp.max(b_last - bcum_row + i_row, axis=1, keepdims=True))
        decay = jnp.exp(b_last + m_prev - m_new)
        w_end = jnp.exp(b_last - bcum_col + i_col - m_new)
        kw = kh.astype(F32) * k_scale * w_end
        c_ref[0, h] = decay * c_prev + lax.dot_general(
            kw.astype(BF16), vh, (((0,), (0,)), ((), ())), preferred_element_type=F32)
        n_ref[0, h] = decay * n_prev + jnp.sum(kw, axis=0, keepdims=True)
        m_ref[0, h] = jnp.broadcast_to(m_new, (1, LANES))

        ms = jnp.mean(h_out * h_out, axis=1, keepdims=True)
        hn = h_out * lax.rsqrt(ms + RMS_EPS) * hn_ref[:, sl]
        y = hn * _sigmoid(og_ref[0, :, sl].astype(F32))
        y_ref[0, :, sl] = (y * _silu(z_ref[0, :, sl].astype(F32))).astype(y_ref.dtype)


def _mlstm(rest, ifc, hnorm_all, layer, c0, n0, m0, chunk):
    b, t, _ = rest.shape
    chunk = min(chunk, t)
    assert t % chunk == 0
    nc = t // chunk
    ifr = jnp.transpose(ifc[:, :, :2 * ML_HEADS].reshape(b, nc, chunk, 2 * ML_HEADS), (0, 1, 3, 2))
    kern = functools.partial(_mlstm_kernel, chunk=chunk)

    def colblock(cb):
        return pl.BlockSpec((1, chunk, BRANCH_W), lambda bi, ci: (bi, ci, cb))

    def state(shape):
        return pl.BlockSpec((1,) + shape, lambda bi, ci: (bi,) + (0,) * len(shape))

    c_shape = (ML_HEADS, ML_HEAD_DIM, ML_HEAD_DIM)
    n_shape = (ML_HEADS, 1, ML_HEAD_DIM)
    m_shape = (ML_HEADS, 1, LANES)
    return pl.pallas_call(
        kern,
        grid=(b, nc),
        in_specs=[colblock(R_ML_Q), colblock(R_ML_K), colblock(R_ML_V), colblock(R_ML_O), colblock(R_Z0 + 2),
                  pl.BlockSpec((1, chunk, LANES), lambda bi, ci: (bi, ci, 0)),
                  pl.BlockSpec((1, 1, 2 * ML_HEADS, chunk), lambda bi, ci: (bi, ci, 0, 0)),
                  pl.BlockSpec((None, 1, BRANCH_W), lambda bi, ci: (layer, 0, 0)),
                  state(c_shape), state(n_shape), state(m_shape)],
        out_specs=[pl.BlockSpec((1, chunk, BRANCH_W), lambda bi, ci: (bi, ci, 0)),
                   state(c_shape), state(n_shape), state(m_shape)],
        out_shape=[jax.ShapeDtypeStruct((b, t, BRANCH_W), BF16),
                   jax.ShapeDtypeStruct((b,) + c_shape, F32),
                   jax.ShapeDtypeStruct((b,) + n_shape, F32),
                   jax.ShapeDtypeStruct((b,) + m_shape, F32)],
        compiler_params=_cparams("parallel", "arbitrary"),
        name="mlstm",
    )(rest, rest, rest, rest, rest, ifc, ifr, hnorm_all, c0, n0, m0)


def _merge_kernel(y0_ref, y1_ref, y2_ref, g0_ref, g1_ref, g2_ref, w_ref, o_ref):
    acc = None
    for bi, (y_ref, g_ref) in enumerate(((y0_ref, g0_ref), (y1_ref, g1_ref), (y2_ref, g2_ref))):
        term = _sigmoid(g_ref[...].astype(F32)) * jnp.dot(y_ref[...], w_ref[bi], preferred_element_type=F32)
        acc = term if acc is None else acc + term
    o_ref[...] = acc.astype(o_ref.dtype)


def _merge(ys, rest2d, w_branch_all, layer, tm=1024, tn=512):
    m = rest2d.shape[0]
    tm = min(tm, m)
    assert m % tm == 0 and D_MODEL % tn == 0
    gb = R_GATE0 * BRANCH_W // tn

    def gate(bi):
        return pl.BlockSpec((tm, tn), lambda i, j: (i, gb + bi * (D_MODEL // tn) + j))

    y_spec = pl.BlockSpec((tm, BRANCH_W), lambda i, j: (i, 0))
    return pl.pallas_call(
        _merge_kernel,
        grid=(m // tm, D_MODEL // tn),
        in_specs=[y_spec, y_spec, y_spec, gate(0), gate(1), gate(2),
                  pl.BlockSpec((None, N_BRANCH, BRANCH_W, tn), lambda i, j: (layer, 0, 0, j))],
        out_specs=pl.BlockSpec((tm, tn), lambda i, j: (i, j)),
        out_shape=jax.ShapeDtypeStruct((m, D_MODEL), BF16),
        compiler_params=_cparams("parallel", "arbitrary"),
        name="merge",
    )(*ys, rest2d, rest2d, rest2d, w_branch_all)


def _out_kernel(a_ref, w_ref, x_ref, gpost_ref, gnext_ref, xo_ref, xn_ref):
    y = jnp.dot(a_ref[...], w_ref[...], preferred_element_type=F32)
    yn = y * lax.rsqrt(jnp.mean(y * y, axis=-1, keepdims=True) + RMS_EPS) * gpost_ref[...]
    xo = x_ref[...] + yn
    xo_ref[...] = xo
    xn = xo * lax.rsqrt(jnp.mean(xo * xo, axis=-1, keepdims=True) + RMS_EPS) * gnext_ref[...]
    xn_ref[...] = xn.astype(xn_ref.dtype)


def _out_proj(merged, w_out_all, x, g_post_all, g_pre_all, layer, tm=256):
    m = x.shape[0]
    tm = min(tm, m)
    assert m % tm == 0
    nxt = (layer + 1) % g_pre_all.shape[0]
    row = pl.BlockSpec((tm, D_MODEL), lambda i: (i, 0))
    return pl.pallas_call(
        _out_kernel,
        grid=(m // tm,),
        in_specs=[row,
                  pl.BlockSpec((None, D_MODEL, D_MODEL), lambda i: (layer, 0, 0)),
                  row,
                  pl.BlockSpec((None, 1, D_MODEL), lambda i: (layer, 0, 0)),
                  pl.BlockSpec((None, 1, D_MODEL), lambda i: (nxt, 0, 0))],
        out_specs=[row, row],
        out_shape=[jax.ShapeDtypeStruct((m, D_MODEL), F32), jax.ShapeDtypeStruct((m, D_MODEL), BF16)],
        compiler_params=_cparams("parallel"),
        name="out_proj",
    )(merged, w_out_all, x, g_post_all, g_pre_all)


def _strict_upper_ones(n):
    r = lax.broadcasted_iota(jnp.int32, (n, n), 0)
    c = lax.broadcasted_iota(jnp.int32, (n, n), 1)
    return (r > c).astype(BF16)


def _trunk_layer(x, xn, shape, pos0, pool_prev16, k_past, v_past, c0, n0, m0, params, layer,
                 sb_tk=256, ml_chunk=64):
    b, t = shape
    wb, w_if, b_if_pad, w_pool, pool_scale, hnorm, w_branch, w_out, g_post, g_pre = params
    (u,) = _mm(xn, wb, layer, WB_POOL, BRANCH_W, (F32,))
    k32, k16 = _mm(xn, wb, layer, WB_SB_K, BRANCH_W, (F32, BF16))
    v32, v16 = _mm(xn, wb, layer, WB_SB_V, BRANCH_W, (F32, BF16))
    (rest,) = _mm(xn, wb, layer, WB_REST, REST_COLS, (BF16,))
    ifc = _if_proj(xn, w_if, b_if_pad, layer)

    u3 = u.reshape(b, t, BRANCH_W)
    rest3 = rest.reshape(b, t, REST_COLS)
    y_pool = _pool(u3, pool_prev16, rest3, w_pool, pool_scale, layer, pos0)

    k16, v16 = k16.reshape(b, t, BRANCH_W), v16.reshape(b, t, BRANCH_W)
    if k_past is not None:
        t_k = pos0 + t
        pad = (-t_k) % sb_tk
        zpad = jnp.zeros((b, pad, BRANCH_W), BF16)
        k16 = jnp.concatenate([k_past.reshape(b, pos0, BRANCH_W).astype(BF16), k16, zpad], axis=1)
        v16 = jnp.concatenate([v_past.reshape(b, pos0, BRANCH_W).astype(BF16), v16, zpad], axis=1)
    y_sb = _stick_breaking(rest3, k16, v16, _strict_upper_ones(sb_tk), pos0)

    y_ml, c_new, n_new, m_new = _mlstm(rest3, ifc.reshape(b, t, LANES), hnorm, layer, c0, n0, m0, ml_chunk)

    ys = [y.reshape(b * t, BRANCH_W) for y in (y_pool, y_sb, y_ml)]
    merged = _merge(ys, rest, w_branch, layer)
    x_out, xn_next = _out_proj(merged, w_out, x, g_post, g_pre, layer)

    pool_new = u3[:, t - POOL_STATE:, :]
    k_sb = k32.reshape(b, t, SB_HEADS, SB_HEAD_DIM)
    v_sb = v32.reshape(b, t, SB_HEADS, SB_HEAD_DIM)
    return x_out, xn_next, (k_sb, v_sb, pool_new, c_new, n_new[:, :, 0, :], m_new[:, :, 0, 0])


def _prep_params(g_pre, w_in, b_if, w_pool, pool_scale, ml_hnorm, w_branch, w_out, g_post):
    depth = w_in.shape[0]
    wb = jnp.concatenate(
        [w_in[:, :, OFF_POOL:OFF_POOL + BRANCH_W], w_in[:, :, OFF_SB_K:OFF_SB_K + BRANCH_W],
         w_in[:, :, OFF_SB_V:OFF_SB_V + BRANCH_W], w_in[:, :, OFF_SB_Q:OFF_SB_Q + BRANCH_W],
         w_in[:, :, OFF_ML_Q:OFF_ML_I], w_in[:, :, OFF_Z:]], axis=2).astype(BF16)
    w_if = jnp.pad(w_in[:, :, OFF_ML_I:OFF_Z], ((0, 0), (0, 0), (0, LANES - 2 * ML_HEADS))).astype(BF16)
    b_if_pad = jnp.pad(b_if.reshape(depth, 1, 2 * ML_HEADS), ((0, 0), (0, 0), (0, LANES - 2 * ML_HEADS)))
    return (wb, w_if, b_if_pad, w_pool.astype(BF16), pool_scale.reshape(depth, 1, BRANCH_W),
            ml_hnorm.reshape(depth, 1, BRANCH_W), w_branch.astype(BF16), w_out.astype(BF16),
            g_post.reshape(depth, 1, D_MODEL), g_pre.reshape(depth, 1, D_MODEL))


def kernel(x_prompt, x_sample, cache_sb_k, cache_sb_v, state_pool, state_ml_c, state_ml_n, state_ml_m,
           g_pre, w_in, b_if, w_pool, pool_scale, ml_hnorm, w_branch, w_out, g_post):
    bp, tp, _ = x_prompt.shape
    bs, ts, _ = x_sample.shape
    past_len = cache_sb_k.shape[2]
    params = _prep_params(g_pre, w_in, b_if, w_pool, pool_scale, ml_hnorm, w_branch, w_out, g_post)
    g_pre3 = params[-1]

    xp = x_prompt.reshape(bp * tp, D_MODEL)
    xs = x_sample.reshape(bs * ts, D_MODEL)
    xnp = _rmsnorm(xp, g_pre3, 0)
    xns = _rmsnorm(xs, g_pre3, 0)

    zero_pool = jnp.zeros((bp, POOL_HALO, BRANCH_W), F32)
    zero_c = jnp.zeros((bp, ML_HEADS, ML_HEAD_DIM, ML_HEAD_DIM), F32)
    zero_n = jnp.zeros((bp, ML_HEADS, 1, ML_HEAD_DIM), F32)
    zero_m = jnp.zeros((bp, ML_HEADS, 1, LANES), F32)
    pool_prev = jnp.pad(state_pool, ((0, 0), (0, 0), (POOL_HALO - POOL_STATE, 0), (0, 0)))
    n_prev = state_ml_n[:, :, :, None, :]
    m_prev = jnp.broadcast_to(state_ml_m[:, :, :, None, None], state_ml_m.shape + (1, LANES))

    pr, sa = [], []
    for l in range(DEPTH):
        xp, xnp, outs_p = _trunk_layer(xp, xnp, (bp, tp), 0, zero_pool, None, None,
                                       zero_c, zero_n, zero_m, params, l)
        xs, xns, outs_s = _trunk_layer(xs, xns, (bs, ts), past_len, pool_prev[l], cache_sb_k[l], cache_sb_v[l],
                                       state_ml_c[l], n_prev[l], m_prev[l], params, l)
        pr.append(outs_p)
        sa.append(outs_s)
    p_k, p_v, p_pool, p_c, p_n, p_m = [jnp.stack(a, axis=0) for a in zip(*pr)]
    s_k, s_v, s_pool, s_c, s_n, s_m = [jnp.stack(a, axis=0) for a in zip(*sa)]
    return (xp.reshape(bp, tp, D_MODEL), xs.reshape(bs, ts, D_MODEL),
            p_k, p_v, p_pool, p_c, p_n, p_m, s_k, s_v, s_pool, s_c, s_n, s_m)
```

```python
import functools
import math

import jax
import jax.numpy as jnp
from jax import lax
from jax.experimental import pallas as pl
from jax.experimental.pallas import tpu as pltpu

F32 = jnp.float32
BF16 = jnp.bfloat16

D_MODEL = 2048
DEPTH = 4
BRANCH_W = 1024
N_BRANCH = 3
POOL_WINDOWS = (2, 4, 8, 16)
POOL_GROUP_W = BRANCH_W // len(POOL_WINDOWS)
POOL_STATE = 15
POOL_HALO = 16
SB_HEADS = 8
SB_HEAD_DIM = BRANCH_W // SB_HEADS
ML_HEADS = 4
ML_HEAD_DIM = BRANCH_W // ML_HEADS
RMS_EPS = 1e-6
LANES = 128

OFF_POOL, OFF_SB_Q, OFF_SB_K, OFF_SB_V = 0, 1024, 2048, 3072
OFF_ML_Q, OFF_ML_K, OFF_ML_V, OFF_ML_O = 4096, 5120, 6144, 7168
OFF_ML_I = 8192
OFF_Z = OFF_ML_I + 2 * ML_HEADS
OFF_GATE = OFF_Z + N_BRANCH * BRANCH_W
IN_COLS = OFF_GATE + N_BRANCH * D_MODEL

ML_COLS = OFF_ML_I - OFF_ML_Q
ZG_COLS = IN_COLS - OFF_Z
ZG_GATE0 = N_BRANCH * BRANCH_W

VMEM_LIMIT = 56 * 1024 * 1024


def _cparams(*sem):
    return pltpu.CompilerParams(dimension_semantics=sem, vmem_limit_bytes=VMEM_LIMIT)


def _sigmoid(x):
    return 1.0 / (1.0 + jnp.exp(-x))


def _silu(x):
    return x * _sigmoid(x)


def _rmsnorm_kernel(x_ref, g_ref, o_ref):
    x = x_ref[...]
    ms = jnp.mean(x * x, axis=-1, keepdims=True)
    o_ref[...] = (x * lax.rsqrt(ms + RMS_EPS) * g_ref[...]).astype(o_ref.dtype)


def _rmsnorm(x, g_all, layer, tm=512):
    n, d = x.shape
    tm = min(tm, n)
    return pl.pallas_call(
        _rmsnorm_kernel,
        grid=(n // tm,),
        in_specs=[pl.BlockSpec((tm, d), lambda i: (i, 0)),
                  pl.BlockSpec((None, 1, d), lambda i: (layer, 0, 0))],
        out_specs=pl.BlockSpec((tm, d), lambda i: (i, 0)),
        out_shape=jax.ShapeDtypeStruct((n, d), BF16),
        compiler_params=_cparams("parallel"),
        name="rmsnorm",
    )(x, g_all)


def _mm_kernel(a_ref, b_ref, *o_refs):
    acc = jnp.dot(a_ref[...], b_ref[...], preferred_element_type=F32)
    for o_ref in o_refs:
        o_ref[...] = acc.astype(o_ref.dtype)


def _mm(a, w_all, layer, col0, ncols, out_dtypes, tm=1024, tn=1024):
    m, k = a.shape
    tm, tn = min(tm, m), min(tn, ncols)
    assert m % tm == 0 and ncols % tn == 0 and col0 % tn == 0
    cb = col0 // tn
    outs = [jax.ShapeDtypeStruct((m, ncols), dt) for dt in out_dtypes]
    res = pl.pallas_call(
        _mm_kernel,
        grid=(m // tm, ncols // tn),
        in_specs=[pl.BlockSpec((tm, k), lambda i, j: (i, 0)),
                  pl.BlockSpec((None, k, tn), lambda i, j: (layer, 0, j + cb))],
        out_specs=[pl.BlockSpec((tm, tn), lambda i, j: (i, j)) for _ in outs],
        out_shape=outs,
        compiler_params=_cparams("parallel", "arbitrary"),
        name="in_proj",
    )(a, w_all)
    return res


def _if_kernel(a_ref, w_ref, b_ref, o_ref):
    acc = jnp.dot(a_ref[...], w_ref[...], preferred_element_type=F32) + b_ref[...]
    col = lax.broadcasted_iota(jnp.int32, acc.shape, 1)
    log_sig = jnp.minimum(acc, 0.0) - jnp.log(1.0 + jnp.exp(-jnp.abs(acc)))
    o_ref[...] = jnp.where(col >= ML_HEADS, log_sig, acc)


def _if_proj(a, w_if, b_if_pad, layer, tm=1024):
    m, k = a.shape
    tm = min(tm, m)
    return pl.pallas_call(
        _if_kernel,
        grid=(m // tm,),
        in_specs=[pl.BlockSpec((tm, k), lambda i: (i, 0)),
                  pl.BlockSpec((None, k, LANES), lambda i: (layer, 0, 0)),
                  pl.BlockSpec((None, 1, LANES), lambda i: (layer, 0, 0))],
        out_specs=pl.BlockSpec((tm, LANES), lambda i: (i, 0)),
        out_shape=jax.ShapeDtypeStruct((m, LANES), F32),
        compiler_params=_cparams("parallel"),
        name="if_proj",
    )(a, w_if, b_if_pad)


def _pool_kernel(u_ref, halo_ref, prev_ref, z_ref, w_ref, s_ref, o_ref, full_ref, *, tt, pos0):
    t = pl.program_id(1)
    full_ref[0:POOL_HALO, :] = jnp.where(t == 0, prev_ref[0], halo_ref[0])
    u = u_ref[0]
    full_ref[POOL_HALO:POOL_HALO + tt, :] = u
    ys = []
    for gi, w in enumerate(POOL_WINDOWS):
        lo = gi * POOL_GROUP_W
        wsum = u[:, lo:lo + POOL_GROUP_W]
        for j in range(1, w):
            wsum = wsum + full_ref[POOL_HALO - j:POOL_HALO - j + tt, lo:lo + POOL_GROUP_W]
        if pos0 + 1 >= w:
            mean = wsum * (1.0 / w)
        else:
            pos = pos0 + t * tt + lax.broadcasted_iota(jnp.int32, (tt, 1), 0)
            mean = wsum / jnp.minimum(pos + 1, w).astype(F32)
        pooled = mean - u[:, lo:lo + POOL_GROUP_W]
        ys.append(jnp.dot(pooled.astype(BF16), w_ref[gi], preferred_element_type=F32))
    y = jnp.concatenate(ys, axis=-1) * s_ref[...]
    o_ref[0] = (y * _silu(z_ref[0].astype(F32))).astype(o_ref.dtype)


def _pool(u, prev16, zg, w_pool_all, scale_all, layer, pos0, tt=512):
    b, t, w = u.shape
    tt = min(tt, t)
    assert t % tt == 0 and tt % POOL_HALO == 0
    hb = tt // POOL_HALO
    kern = functools.partial(_pool_kernel, tt=tt, pos0=pos0)
    return pl.pallas_call(
        kern,
        grid=(b, t // tt),
        in_specs=[pl.BlockSpec((1, tt, w), lambda i, j: (i, j, 0)),
                  pl.BlockSpec((1, POOL_HALO, w), lambda i, j: (i, jnp.maximum(j * hb - 1, 0), 0)),
                  pl.BlockSpec((1, POOL_HALO, w), lambda i, j: (i, 0, 0)),
                  pl.BlockSpec((1, tt, w), lambda i, j: (i, j, 0)),
                  pl.BlockSpec((None, len(POOL_WINDOWS), POOL_GROUP_W, POOL_GROUP_W), lambda i, j: (layer, 0, 0, 0)),
                  pl.BlockSpec((None, 1, w), lambda i, j: (layer, 0, 0))],
        out_specs=pl.BlockSpec((1, tt, w), lambda i, j: (i, j, 0)),
        out_shape=jax.ShapeDtypeStruct((b, t, w), BF16),
        scratch_shapes=[pltpu.VMEM((POOL_HALO + tt, w), F32)],
        compiler_params=_cparams("parallel", "arbitrary"),
        name="pool",
    )(u, u, prev16, zg, w_pool_all, scale_all)


_NT = (((1,), (1,)), ((), ()))
_SIGN_BIT = -2 ** 31
LOG2_E = 1.4426950408889634


def _sb_kernel(q_ref, k_ref, v_ref, z_ref, tri_ref, o_ref, acc_ref, carry_ref, *, tq, tk, q_pos0, heads):
    i = pl.program_id(2)
    q_start = q_pos0 + i * tq
    n_full = q_start // tk
    tri = tri_ref[...]

    def lanes(h):
        return slice(h * SB_HEAD_DIM, (h + 1) * SB_HEAD_DIM)

    def blocks(kb, valid=None):
        k0 = pl.multiple_of(kb * tk, tk)
        hs = range(heads)
        z2 = [lax.dot_general(q_ref[0, :, lanes(h)], k_ref[0, pl.ds(k0, tk), lanes(h)].astype(BF16), _NT,
                              preferred_element_type=F32) for h in hs]
        suffix = []
        for h in hs:
            neg_abs = lax.bitcast_convert_type(lax.bitcast_convert_type(z2[h], jnp.int32) | _SIGN_BIT, F32)
            sp2 = jnp.maximum(z2[h], 0.0) + jnp.log(1.0 + jnp.exp2(neg_abs)) * LOG2_E
            if valid is not None:
                sp2 = jnp.where(valid, sp2, 0.0)
            hi = sp2.astype(BF16)
            lo = (sp2 - hi.astype(F32)).astype(BF16)
            suffix.append(jnp.dot(jnp.concatenate([hi, lo], axis=1), tri, preferred_element_type=F32))
        out = []
        for h in hs:
            pre = z2[h] - suffix[h]
            if valid is not None:
                pre = jnp.where(valid, pre, -jnp.inf)
            pv = jnp.dot(jnp.exp2(pre).astype(BF16), v_ref[0, pl.ds(k0, tk), lanes(h)].astype(BF16),
                         preferred_element_type=F32)
            out.append((pv, suffix[h][:, 0:1]))
        return out

    qpos = q_start + lax.broadcasted_iota(jnp.int32, (tq, tk), 0)
    kpos = n_full * tk + lax.broadcasted_iota(jnp.int32, (tq, tk), 1)
    for h, (pv, blocksum) in enumerate(blocks(n_full, kpos < qpos)):
        acc_ref[h], carry_ref[h] = pv, blocksum

    @pl.loop(0, n_full)
    def _(t):
        for h, (pv, blocksum) in enumerate(blocks(n_full - 1 - t)):
            carry = carry_ref[h]
            acc_ref[h] += jnp.exp2(-carry) * pv
            carry_ref[h] = carry + blocksum

    for h in range(heads):
        o_ref[0, :, lanes(h)] = (acc_ref[h] * _silu(z_ref[0, :, lanes(h)].astype(F32))).astype(o_ref.dtype)


def _stick_breaking(q, zg, k_all, v_all, tri, q_pos0, tq, heads):
    b, t_q, _ = q.shape
    t_k = k_all.shape[1]
    tk = tri.shape[1]
    tq = min(tq, t_q)
    assert t_q % tq == 0 and t_k % tk == 0 and t_k >= q_pos0 + t_q
    assert tk % tq == 0 and q_pos0 % tq == 0 and SB_HEADS % heads == 0
    w = heads * SB_HEAD_DIM
    hz = BRANCH_W // w
    kern = functools.partial(_sb_kernel, tq=tq, tk=tk, q_pos0=q_pos0, heads=heads)
    return pl.pallas_call(
        kern,
        grid=(b, SB_HEADS // heads, t_q // tq),
        in_specs=[pl.BlockSpec((1, tq, w), lambda bi, h, i: (bi, i, h)),
                  pl.BlockSpec((1, t_k, w), lambda bi, h, i: (bi, 0, h)),
                  pl.BlockSpec((1, t_k, w), lambda bi, h, i: (bi, 0, h)),
                  pl.BlockSpec((1, tq, w), lambda bi, h, i: (bi, i, hz + h)),
                  pl.BlockSpec((2 * tk, tk), lambda bi, h, i: (0, 0))],
        out_specs=pl.BlockSpec((1, tq, w), lambda bi, h, i: (bi, i, h)),
        out_shape=jax.ShapeDtypeStruct((b, t_q, BRANCH_W), BF16),
        scratch_shapes=[pltpu.VMEM((heads, tq, SB_HEAD_DIM), F32), pltpu.VMEM((heads, tq, 1), F32)],
        compiler_params=_cparams("parallel", "parallel", "arbitrary"),
        name="stick_breaking",
    )(q, k_all, v_all, zg, tri)


def _mlstm_kernel(q_ref, k_ref, v_ref, og_ref, z_ref, ifc_ref, ifr_ref, hn_ref, c0_ref, n0_ref, m0_ref,
                  y_ref, c_ref, n_ref, m_ref, *, chunk):
    ci = pl.program_id(1)

    @pl.when(ci == 0)
    def _():
        c_ref[...] = c0_ref[...]
        n_ref[...] = n0_ref[...]
        m_ref[...] = m0_ref[...]

    row = lax.broadcasted_iota(jnp.int32, (chunk, chunk), 0)
    col = lax.broadcasted_iota(jnp.int32, (chunk, chunk), 1)
    causal = col <= row
    log_k_scale = -0.5 * math.log(ML_HEAD_DIM)
    hs = range(ML_HEADS)
    sls = [slice(h * ML_HEAD_DIM, (h + 1) * ML_HEAD_DIM) for h in hs]
    q = [q_ref[0, :, sl] for sl in sls]
    k = [k_ref[0, :, sl] for sl in sls]
    v = [v_ref[0, :, sl] for sl in sls]
    c_prev = [c_ref[0, h] for h in hs]
    n_prev = [n_ref[0, h] for h in hs]
    m_prev = [m_ref[0, h, :, 0:1] for h in hs]
    qk = [lax.dot_general(q[h], k[h], _NT, preferred_element_type=F32) for h in hs]
    q_c = [jnp.dot(q[h], c_prev[h].astype(BF16), preferred_element_type=F32) for h in hs]

    f_col = [ifc_ref[0, :, ML_HEADS + h:ML_HEADS + h + 1] for h in hs]
    f_row = [ifr_ref[0, 0, ML_HEADS + h:ML_HEADS + h + 1, :] for h in hs]
    i_row = [ifr_ref[0, 0, h:h + 1, :] for h in hs]
    bcum_col = [jnp.sum(jnp.where(causal, f_row[h], 0.0), axis=1, keepdims=True) for h in hs]
    bcum_row = [jnp.sum(jnp.where(row <= col, f_col[h], 0.0), axis=0, keepdims=True) for h in hs]
    d = [jnp.where(causal, bcum_col[h] - bcum_row[h] + i_row[h], -jnp.inf) for h in hs]
    g = [bcum_col[h] + m_prev[h] for h in hs]
    m_t = [jnp.maximum(g[h], jnp.max(d[h], axis=1, keepdims=True)) for h in hs]
    w_inter = [jnp.exp(g[h] - m_t[h]) for h in hs]
    s = [qk[h] * jnp.exp(d[h] - (m_t[h] - log_k_scale)) for h in hs]
    sv = [jnp.dot(s[h].astype(BF16), v[h], preferred_element_type=F32) for h in hs]

    i_col = [ifc_ref[0, :, h:h + 1] for h in hs]
    b_last = [bcum_row[h][:, chunk - 1:chunk] for h in hs]
    m_new = [jnp.maximum(b_last[h] + m_prev[h],
                         jnp.max(b_last[h] - bcum_row[h] + i_row[h], axis=1, keepdims=True)) for h in hs]
    decay = [jnp.exp(b_last[h] + m_prev[h] - m_new[h]) for h in hs]
    w_end = [jnp.exp(b_last[h] - bcum_col[h] + i_col[h] - (m_new[h] - log_k_scale)) for h in hs]
    kw = [k[h].astype(F32) * w_end[h] for h in hs]
    kv = [lax.dot_general(kw[h].astype(BF16), v[h], (((0,), (0,)), ((), ())), preferred_element_type=F32)
          for h in hs]

    q_n = [jnp.sum(q[h].astype(F32) * n_prev[h], axis=1, keepdims=True) for h in hs]
    den = [jnp.sum(s[h], axis=1, keepdims=True) + w_inter[h] * q_n[h] for h in hs]
    h_out = [(sv[h] + w_inter[h] * q_c[h]) / jnp.maximum(jnp.abs(den[h]), jnp.exp(-m_t[h])) for h in hs]
    ms = [jnp.mean(h_out[h] * h_out[h], axis=1, keepdims=True) for h in hs]
    for h in hs:
        hn = h_out[h] * lax.rsqrt(ms[h] + RMS_EPS) * hn_ref[:, sls[h]]
        y = hn * _sigmoid(og_ref[0, :, sls[h]].astype(F32))
        y_ref[0, :, sls[h]] = (y * _silu(z_ref[0, :, sls[h]].astype(F32))).astype(y_ref.dtype)

    for h in hs:
        c_ref[0, h] = decay[h] * c_prev[h] + kv[h]
        n_ref[0, h] = decay[h] * n_prev[h] + jnp.sum(kw[h], axis=0, keepdims=True)
        m_ref[0, h] = jnp.broadcast_to(m_new[h], (1, LANES))


def _mlstm(ml, zg, ifc, hnorm_all, layer, c0, n0, m0, chunk):
    b, t, _ = ml.shape
    chunk = min(chunk, t)
    assert t % chunk == 0
    nc = t // chunk
    ifr = jnp.transpose(ifc[:, :, :2 * ML_HEADS].reshape(b, nc, chunk, 2 * ML_HEADS), (0, 1, 3, 2))
    kern = functools.partial(_mlstm_kernel, chunk=chunk)

    def colblock(cb):
        return pl.BlockSpec((1, chunk, BRANCH_W), lambda bi, ci: (bi, ci, cb))

    def state(shape):
        return pl.BlockSpec((1,) + shape, lambda bi, ci: (bi,) + (0,) * len(shape))

    c_shape = (ML_HEADS, ML_HEAD_DIM, ML_HEAD_DIM)
    n_shape = (ML_HEADS, 1, ML_HEAD_DIM)
    m_shape = (ML_HEADS, 1, LANES)
    return pl.pallas_call(
        kern,
        grid=(b, nc),
        in_specs=[colblock(0), colblock(1), colblock(2), colblock(3), colblock(2),
                  pl.BlockSpec((1, chunk, LANES), lambda bi, ci: (bi, ci, 0)),
                  pl.BlockSpec((1, 1, 2 * ML_HEADS, chunk), lambda bi, ci: (bi, ci, 0, 0)),
                  pl.BlockSpec((None, 1, BRANCH_W), lambda bi, ci: (layer, 0, 0)),
                  state(c_shape), state(n_shape), state(m_shape)],
        out_specs=[pl.BlockSpec((1, chunk, BRANCH_W), lambda bi, ci: (bi, ci, 0)),
                   state(c_shape), state(n_shape), state(m_shape)],
        out_shape=[jax.ShapeDtypeStruct((b, t, BRANCH_W), BF16),
                   jax.ShapeDtypeStruct((b,) + c_shape, F32),
                   jax.ShapeDtypeStruct((b,) + n_shape, F32),
                   jax.ShapeDtypeStruct((b,) + m_shape, F32)],
        compiler_params=_cparams("parallel", "arbitrary"),
        name="mlstm",
    )(ml, ml, ml, ml, zg, ifc, ifr, hnorm_all, c0, n0, m0)


def _merge_kernel(y0_ref, y1_ref, y2_ref, g0_ref, g1_ref, g2_ref, w_ref, o_ref):
    acc = None
    for bi, (y_ref, g_ref) in enumerate(((y0_ref, g0_ref), (y1_ref, g1_ref), (y2_ref, g2_ref))):
        term = _sigmoid(g_ref[...].astype(F32)) * jnp.dot(y_ref[...], w_ref[bi], preferred_element_type=F32)
        acc = term if acc is None else acc + term
    o_ref[...] = acc.astype(o_ref.dtype)


def _merge(ys, zg2d, w_branch_all, layer, tm=1024, tn=512):
    m = zg2d.shape[0]
    tm = min(tm, m)
    assert m % tm == 0 and D_MODEL % tn == 0 and ZG_GATE0 % tn == 0
    gb = ZG_GATE0 // tn

    def gate(bi):
        return pl.BlockSpec((tm, tn), lambda i, j: (i, gb + bi * (D_MODEL // tn) + j))

    y_spec = pl.BlockSpec((tm, BRANCH_W), lambda i, j: (i, 0))
    return pl.pallas_call(
        _merge_kernel,
        grid=(m // tm, D_MODEL // tn),
        in_specs=[y_spec, y_spec, y_spec, gate(0), gate(1), gate(2),
                  pl.BlockSpec((None, N_BRANCH, BRANCH_W, tn), lambda i, j: (layer, 0, 0, j))],
        out_specs=pl.BlockSpec((tm, tn), lambda i, j: (i, j)),
        out_shape=jax.ShapeDtypeStruct((m, D_MODEL), BF16),
        compiler_params=_cparams("parallel", "arbitrary"),
        name="merge",
    )(*ys, zg2d, zg2d, zg2d, w_branch_all)


def _out_kernel(a_ref, w_ref, x_ref, gpost_ref, gnext_ref, xo_ref, xn_ref):
    y = jnp.dot(a_ref[...], w_ref[...], preferred_element_type=F32)
    yn = y * lax.rsqrt(jnp.mean(y * y, axis=-1, keepdims=True) + RMS_EPS) * gpost_ref[...]
    xo = x_ref[...] + yn
    xo_ref[...] = xo
    xn = xo * lax.rsqrt(jnp.mean(xo * xo, axis=-1, keepdims=True) + RMS_EPS) * gnext_ref[...]
    xn_ref[...] = xn.astype(xn_ref.dtype)


def _out_proj(merged, w_out_all, x, g_post_all, g_pre_all, layer, tm=256):
    m = x.shape[0]
    tm = min(tm, m)
    assert m % tm == 0
    nxt = (layer + 1) % g_pre_all.shape[0]
    row = pl.BlockSpec((tm, D_MODEL), lambda i: (i, 0))
    return pl.pallas_call(
        _out_kernel,
        grid=(m // tm,),
        in_specs=[row,
                  pl.BlockSpec((None, D_MODEL, D_MODEL), lambda i: (layer, 0, 0)),
                  row,
                  pl.BlockSpec((None, 1, D_MODEL), lambda i: (layer, 0, 0)),
                  pl.BlockSpec((None, 1, D_MODEL), lambda i: (nxt, 0, 0))],
        out_specs=[row, row],
        out_shape=[jax.ShapeDtypeStruct((m, D_MODEL), F32), jax.ShapeDtypeStruct((m, D_MODEL), BF16)],
        compiler_params=_cparams("parallel"),
        name="out_proj",
    )(merged, w_out_all, x, g_post_all, g_pre_all)


def _suffix_sum_matrix(n):
    r = lax.broadcasted_iota(jnp.int32, (2 * n, n), 0) % n
    c = lax.broadcasted_iota(jnp.int32, (2 * n, n), 1)
    return (r >= c).astype(BF16)


def _trunk_layer(x, xn, shape, pos0, pool_prev16, k_past, v_past, c0, n0, m0, params, layer,
                 sb_tk=256, sb_tq=256, sb_heads=8, ml_chunk=256):
    b, t = shape
    wa, wz, w_if, b_if_pad, w_pool, pool_scale, hnorm, w_branch, w_out, g_post, g_pre = params
    (u,) = _mm(xn, wa, layer, OFF_POOL, BRANCH_W, (F32,))
    (q16,) = _mm(xn, wa, layer, OFF_SB_Q, BRANCH_W, (BF16,))
    k32, k16 = _mm(xn, wa, layer, OFF_SB_K, BRANCH_W, (F32, BF16))
    v32, v16 = _mm(xn, wa, layer, OFF_SB_V, BRANCH_W, (F32, BF16))
    (ml,) = _mm(xn, wa, layer, OFF_ML_Q, ML_COLS, (BF16,))
    (zg,) = _mm(xn, wz, layer, 0, ZG_COLS, (BF16,))
    ifc = _if_proj(xn, w_if, b_if_pad, layer)

    u3 = u.reshape(b, t, BRANCH_W)
    zg3 = zg.reshape(b, t, ZG_COLS)
    y_pool = _pool(u3, pool_prev16, zg3, w_pool, pool_scale, layer, pos0)

    k16, v16 = k16.reshape(b, t, BRANCH_W), v16.reshape(b, t, BRANCH_W)
    if k_past is not None:
        t_k = pos0 + t
        pad = (-t_k) % sb_tk
        zpad = jnp.zeros((b, pad, BRANCH_W), BF16)
        k16 = jnp.concatenate([k_past.reshape(b, pos0, BRANCH_W).astype(BF16), k16, zpad], axis=1)
        v16 = jnp.concatenate([v_past.reshape(b, pos0, BRANCH_W).astype(BF16), v16, zpad], axis=1)
    y_sb = _stick_breaking(q16.reshape(b, t, BRANCH_W), zg3, k16, v16, _suffix_sum_matrix(sb_tk), pos0,
                           sb_tq, sb_heads)

    y_ml, c_new, n_new, m_new = _mlstm(ml.reshape(b, t, ML_COLS), zg3, ifc.reshape(b, t, LANES), hnorm, layer,
                                       c0, n0, m0, ml_chunk)

    ys = [y.reshape(b * t, BRANCH_W) for y in (y_pool, y_sb, y_ml)]
    merged = _merge(ys, zg, w_branch, layer)
    x_out, xn_next = _out_proj(merged, w_out, x, g_post, g_pre, layer)

    pool_new = u3[:, t - POOL_STATE:, :]
    k_sb = k32.reshape(b, t, SB_HEADS, SB_HEAD_DIM)
    v_sb = v32.reshape(b, t, SB_HEADS, SB_HEAD_DIM)
    return x_out, xn_next, (k_sb, v_sb, pool_new, c_new, n_new[:, :, 0, :], m_new[:, :, 0, 0])


def _cast_kernel(x_ref, o_ref):
    o_ref[...] = x_ref[...].astype(o_ref.dtype)


def _scaled_cast_kernel(x_ref, s_ref, o_ref):
    o_ref[...] = (x_ref[...] * s_ref[...]).astype(o_ref.dtype)


def _cast_bf16(x, ncols, col_scale=None, tr=512, tc=2048):
    d, r, _ = x.shape
    tr = min(tr, r)
    assert r % tr == 0 and ncols % tc == 0
    blk = pl.BlockSpec((None, tr, tc), lambda i, j, k: (i, j, k))
    in_specs, args, kern = [blk], [x], _cast_kernel
    if col_scale is not None:
        in_specs.append(pl.BlockSpec((1, tc), lambda i, j, k: (0, k)))
        args.append(col_scale)
        kern = _scaled_cast_kernel
    return pl.pallas_call(
        kern,
        grid=(d, r // tr, ncols // tc),
        in_specs=in_specs,
        out_specs=blk,
        out_shape=jax.ShapeDtypeStruct((d, r, ncols), BF16),
        compiler_params=_cparams("parallel", "parallel", "parallel"),
        name="cast_bf16",
    )(*args)


def _prep_params(g_pre, w_in, b_if, w_pool, pool_scale, ml_hnorm, w_branch, w_out, g_post):
    depth = w_in.shape[0]
    q_scale = SB_HEAD_DIM ** -0.5 * LOG2_E
    cols = lax.broadcasted_iota(jnp.int32, (1, OFF_ML_I), 1)
    col_scale = jnp.where((cols >= OFF_SB_Q) & (cols < OFF_SB_K), q_scale, 1.0).astype(F32)
    wa = _cast_bf16(w_in, OFF_ML_I, col_scale)
    wz = w_in[:, :, OFF_Z:].astype(BF16)
    w_if = jnp.pad(w_in[:, :, OFF_ML_I:OFF_Z], ((0, 0), (0, 0), (0, LANES - 2 * ML_HEADS))).astype(BF16)
    b_if_pad = jnp.pad(b_if.reshape(depth, 1, 2 * ML_HEADS), ((0, 0), (0, 0), (0, LANES - 2 * ML_HEADS)))
    w_branch16 = _cast_bf16(w_branch.reshape(depth * N_BRANCH, BRANCH_W, D_MODEL), D_MODEL)
    w_out16 = _cast_bf16(w_out, D_MODEL)
    return (wa, wz, w_if, b_if_pad, w_pool.astype(BF16), pool_scale.reshape(depth, 1, BRANCH_W),
            ml_hnorm.reshape(depth, 1, BRANCH_W), w_branch16.reshape(depth, N_BRANCH, BRANCH_W, D_MODEL), w_out16,
            g_post.reshape(depth, 1, D_MODEL), g_pre.reshape(depth, 1, D_MODEL))


def kernel(x_prompt, x_sample, cache_sb_k, cache_sb_v, state_pool, state_ml_c, state_ml_n, state_ml_m,
           g_pre, w_in, b_if, w_pool, pool_scale, ml_hnorm, w_branch, w_out, g_post):
    bp, tp, _ = x_prompt.shape
    bs, ts, _ = x_sample.shape
    past_len = cache_sb_k.shape[2]
    params = _prep_params(g_pre, w_in, b_if, w_pool, pool_scale, ml_hnorm, w_branch, w_out, g_post)
    g_pre3 = params[-1]

    xp = x_prompt.reshape(bp * tp, D_MODEL)
    xs = x_sample.reshape(bs * ts, D_MODEL)
    xnp = _rmsnorm(xp, g_pre3, 0)
    xns = _rmsnorm(xs, g_pre3, 0)

    zero_pool = jnp.zeros((bp, POOL_HALO, BRANCH_W), F32)
    zero_c = jnp.zeros((bp, ML_HEADS, ML_HEAD_DIM, ML_HEAD_DIM), F32)
    zero_n = jnp.zeros((bp, ML_HEADS, 1, ML_HEAD_DIM), F32)
    zero_m = jnp.zeros((bp, ML_HEADS, 1, LANES), F32)
    pool_prev = jnp.pad(state_pool, ((0, 0), (0, 0), (POOL_HALO - POOL_STATE, 0), (0, 0)))
    n_prev = state_ml_n[:, :, :, None, :]
    m_prev = jnp.broadcast_to(state_ml_m[:, :, :, None, None], state_ml_m.shape + (1, LANES))

    pr, sa = [], []
    for l in range(DEPTH):
        xp, xnp, outs_p = _trunk_layer(xp, xnp, (bp, tp), 0, zero_pool, None, None,
                                       zero_c, zero_n, zero_m, params, l)
        xs, xns, outs_s = _trunk_layer(xs, xns, (bs, ts), past_len, pool_prev[l], cache_sb_k[l], cache_sb_v[l],
                                       state_ml_c[l], n_prev[l], m_prev[l], params, l, sb_tq=64, sb_heads=4)
        pr.append(outs_p)
        sa.append(outs_s)
    p_k, p_v, p_pool, p_c, p_n, p_m = [jnp.stack(a, axis=0) for a in zip(*pr)]
    s_k, s_v, s_pool, s_c, s_n, s_m = [jnp.stack(a, axis=0) for a in zip(*sa)]
    return (xp.reshape(bp, tp, D_MODEL), xs.reshape(bs, ts, D_MODEL),
            p_k, p_v, p_pool, p_c, p_n, p_m, s_k, s_v, s_pool, s_c, s_n, s_m)
```

```python
import functools
import math

import jax
import jax.numpy as jnp
from jax import lax
from jax.experimental import pallas as pl
from jax.experimental.pallas import tpu as pltpu

F32 = jnp.float32
BF16 = jnp.bfloat16

D_MODEL = 2048
DEPTH = 4
BRANCH_W = 1024
N_BRANCH = 3
POOL_WINDOWS = (2, 4, 8, 16)
POOL_GROUP_W = BRANCH_W // len(POOL_WINDOWS)
POOL_STATE = 15
POOL_HALO = 16
SB_HEADS = 8
SB_HEAD_DIM = BRANCH_W // SB_HEADS
ML_HEADS = 4
ML_HEAD_DIM = BRANCH_W // ML_HEADS
RMS_EPS = 1e-6
LANES = 128

OFF_POOL, OFF_SB_Q, OFF_SB_K, OFF_SB_V = 0, 1024, 2048, 3072
OFF_ML_Q, OFF_ML_K, OFF_ML_V, OFF_ML_O = 4096, 5120, 6144, 7168
OFF_ML_I = 8192
OFF_Z = OFF_ML_I + 2 * ML_HEADS
OFF_GATE = OFF_Z + N_BRANCH * BRANCH_W
IN_COLS = OFF_GATE + N_BRANCH * D_MODEL

ML_COLS = OFF_ML_I - OFF_ML_Q
ZG_COLS = IN_COLS - OFF_Z
ZG_GATE0 = N_BRANCH * BRANCH_W

VMEM_LIMIT = 56 * 1024 * 1024


def _cparams(*sem):
    return pltpu.CompilerParams(dimension_semantics=sem, vmem_limit_bytes=VMEM_LIMIT)


def _sigmoid(x):
    return 1.0 / (1.0 + jnp.exp(-x))


def _silu(x):
    return x * _sigmoid(x)


def _rmsnorm_kernel(x_ref, g_ref, o_ref):
    x = x_ref[...]
    ms = jnp.mean(x * x, axis=-1, keepdims=True)
    o_ref[...] = (x * lax.rsqrt(ms + RMS_EPS) * g_ref[...]).astype(o_ref.dtype)


def _rmsnorm(x, g_all, layer, tm=512):
    n, d = x.shape
    tm = min(tm, n)
    return pl.pallas_call(
        _rmsnorm_kernel,
        grid=(n // tm,),
        in_specs=[pl.BlockSpec((tm, d), lambda i: (i, 0)),
                  pl.BlockSpec((None, 1, d), lambda i: (layer, 0, 0))],
        out_specs=pl.BlockSpec((tm, d), lambda i: (i, 0)),
        out_shape=jax.ShapeDtypeStruct((n, d), BF16),
        compiler_params=_cparams("parallel"),
        name="rmsnorm",
    )(x, g_all)


def _mm_kernel(a_ref, b_ref, *o_refs):
    acc = jnp.dot(a_ref[...], b_ref[...], preferred_element_type=F32)
    for o_ref in o_refs:
        o_ref[...] = acc.astype(o_ref.dtype)


def _mm(a, w_all, layer, col0, ncols, out_dtypes, tm=1024, tn=1024):
    m, k = a.shape
    tm, tn = min(tm, m), min(tn, ncols)
    assert m % tm == 0 and ncols % tn == 0 and col0 % tn == 0
    cb = col0 // tn
    outs = [jax.ShapeDtypeStruct((m, ncols), dt) for dt in out_dtypes]
    res = pl.pallas_call(
        _mm_kernel,
        grid=(m // tm, ncols // tn),
        in_specs=[pl.BlockSpec((tm, k), lambda i, j: (i, 0)),
                  pl.BlockSpec((None, k, tn), lambda i, j: (layer, 0, j + cb))],
        out_specs=[pl.BlockSpec((tm, tn), lambda i, j: (i, j)) for _ in outs],
        out_shape=outs,
        compiler_params=_cparams("parallel", "arbitrary"),
        name="in_proj",
    )(a, w_all)
    return res


def _mm_heads_kernel(a_ref, b_ref, stacked_ref, o32_ref, o16_ref):
    del stacked_ref
    acc = jnp.dot(a_ref[...], b_ref[...], preferred_element_type=F32)
    tm = acc.shape[0]
    for h in range(SB_HEADS):
        o32_ref[pl.ds(h, tm, stride=SB_HEADS), :] = acc[:, h * SB_HEAD_DIM:(h + 1) * SB_HEAD_DIM]
    o16_ref[...] = acc.astype(o16_ref.dtype)


def _mm_heads(a, w_all, layer, col0, stacked, tm=1024):
    m, k = a.shape
    tm = min(tm, m)
    assert m % tm == 0 and col0 % BRANCH_W == 0 and stacked.shape[1:] == (m * SB_HEADS, SB_HEAD_DIM)
    cb = col0 // BRANCH_W
    return pl.pallas_call(
        _mm_heads_kernel,
        grid=(m // tm,),
        in_specs=[pl.BlockSpec((tm, k), lambda i: (i, 0)),
                  pl.BlockSpec((None, k, BRANCH_W), lambda i: (layer, 0, cb)),
                  pl.BlockSpec(memory_space=pl.ANY)],
        out_specs=[pl.BlockSpec((None, tm * SB_HEADS, SB_HEAD_DIM), lambda i: (layer, i, 0)),
                   pl.BlockSpec((tm, BRANCH_W), lambda i: (i, 0))],
        out_shape=[jax.ShapeDtypeStruct(stacked.shape, F32),
                   jax.ShapeDtypeStruct((m, BRANCH_W), BF16)],
        input_output_aliases={2: 0},
        compiler_params=_cparams("parallel"),
        name="in_proj_heads",
    )(a, w_all, stacked)


def _if_kernel(a_ref, w_ref, b_ref, o_ref):
    acc = jnp.dot(a_ref[...], w_ref[...], preferred_element_type=F32) + b_ref[...]
    col = lax.broadcasted_iota(jnp.int32, acc.shape, 1)
    log_sig = jnp.minimum(acc, 0.0) - jnp.log(1.0 + jnp.exp(-jnp.abs(acc)))
    o_ref[...] = jnp.where(col >= ML_HEADS, log_sig, acc)


def _if_proj(a, w_if, b_if_pad, layer, tm=1024):
    m, k = a.shape
    tm = min(tm, m)
    return pl.pallas_call(
        _if_kernel,
        grid=(m // tm,),
        in_specs=[pl.BlockSpec((tm, k), lambda i: (i, 0)),
                  pl.BlockSpec((None, k, LANES), lambda i: (layer, 0, 0)),
                  pl.BlockSpec((None, 1, LANES), lambda i: (layer, 0, 0))],
        out_specs=pl.BlockSpec((tm, LANES), lambda i: (i, 0)),
        out_shape=jax.ShapeDtypeStruct((m, LANES), F32),
        compiler_params=_cparams("parallel"),
        name="if_proj",
    )(a, w_if, b_if_pad)


def _pool_kernel(u_ref, halo_ref, prev_ref, z_ref, w_ref, s_ref, o_ref, full_ref, *, tt, pos0):
    t = pl.program_id(1)
    full_ref[0:POOL_HALO, :] = jnp.where(t == 0, prev_ref[0], halo_ref[0])
    u = u_ref[0]
    full_ref[POOL_HALO:POOL_HALO + tt, :] = u
    ys = []
    for gi, w in enumerate(POOL_WINDOWS):
        lo = gi * POOL_GROUP_W
        wsum = u[:, lo:lo + POOL_GROUP_W]
        for j in range(1, w):
            wsum = wsum + full_ref[POOL_HALO - j:POOL_HALO - j + tt, lo:lo + POOL_GROUP_W]
        if pos0 + 1 >= w:
            mean = wsum * (1.0 / w)
        else:
            pos = pos0 + t * tt + lax.broadcasted_iota(jnp.int32, (tt, 1), 0)
            mean = wsum / jnp.minimum(pos + 1, w).astype(F32)
        pooled = mean - u[:, lo:lo + POOL_GROUP_W]
        ys.append(jnp.dot(pooled.astype(BF16), w_ref[gi], preferred_element_type=F32))
    y = jnp.concatenate(ys, axis=-1) * s_ref[...]
    o_ref[0] = (y * _silu(z_ref[0].astype(F32))).astype(o_ref.dtype)


def _pool(u, prev16, zg, w_pool_all, scale_all, layer, pos0, tt=512):
    b, t, w = u.shape
    tt = min(tt, t)
    assert t % tt == 0 and tt % POOL_HALO == 0
    hb = tt // POOL_HALO
    kern = functools.partial(_pool_kernel, tt=tt, pos0=pos0)
    return pl.pallas_call(
        kern,
        grid=(b, t // tt),
        in_specs=[pl.BlockSpec((1, tt, w), lambda i, j: (i, j, 0)),
                  pl.BlockSpec((1, POOL_HALO, w), lambda i, j: (i, jnp.maximum(j * hb - 1, 0), 0)),
                  pl.BlockSpec((1, POOL_HALO, w), lambda i, j: (i, 0, 0)),
                  pl.BlockSpec((1, tt, w), lambda i, j: (i, j, 0)),
                  pl.BlockSpec((None, len(POOL_WINDOWS), POOL_GROUP_W, POOL_GROUP_W), lambda i, j: (layer, 0, 0, 0)),
                  pl.BlockSpec((None, 1, w), lambda i, j: (layer, 0, 0))],
        out_specs=pl.BlockSpec((1, tt, w), lambda i, j: (i, j, 0)),
        out_shape=jax.ShapeDtypeStruct((b, t, w), BF16),
        scratch_shapes=[pltpu.VMEM((POOL_HALO + tt, w), F32)],
        compiler_params=_cparams("parallel", "arbitrary"),
        name="pool",
    )(u, u, prev16, zg, w_pool_all, scale_all)


_NT = (((1,), (1,)), ((), ()))
_SIGN_BIT = -2 ** 31
LOG2_E = 1.4426950408889634


def _sb_kernel(q_ref, kd_ref, vd_ref, k_ref, v_ref, z_ref, trid_ref, tri_ref, o_ref, acc_ref, carry_ref, *,
               tq, tk, q_pos0, heads, cache_rows):
    i = pl.program_id(2)
    n_full = (q_pos0 + i * tq) // tk
    hs = range(heads)

    def lanes(h):
        return slice(h * SB_HEAD_DIM, (h + 1) * SB_HEAD_DIM)

    def scores(k_of):
        return [lax.dot_general(q_ref[0, :, lanes(h)], k_of(h), _NT, preferred_element_type=F32) for h in hs]

    def weigh(z2, v_of, tri, carry=None, valid=None):
        suffix = []
        for h in hs:
            neg_abs = lax.bitcast_convert_type(lax.bitcast_convert_type(z2[h], jnp.int32) | _SIGN_BIT, F32)
            sp2 = jnp.maximum(z2[h], 0.0) + jnp.log(1.0 + jnp.exp2(neg_abs)) * LOG2_E
            if valid is not None:
                sp2 = jnp.where(valid, sp2, 0.0)
            hi = sp2.astype(BF16)
            lo = (sp2 - hi.astype(F32)).astype(BF16)
            suffix.append(jnp.dot(jnp.concatenate([hi, lo], axis=1), tri, preferred_element_type=F32))
        out = []
        for h in hs:
            pre = z2[h] - suffix[h]
            if carry is not None:
                pre = pre - carry[h]
            if valid is not None:
                pre = jnp.where(valid, pre, -jnp.inf)
            pv = jnp.dot(jnp.exp2(pre).astype(BF16), v_of(h), preferred_element_type=F32)
            out.append((pv, suffix[h][:, 0:1]))
        return out

    def earlier(ref, kb):
        k0 = pl.multiple_of(kb * tk, tk)
        if cache_rows:
            base = pl.multiple_of(k0 * heads, tk * heads)
            return lambda h: ref[0, pl.ds(base + h, tk, stride=heads), :].astype(BF16)
        return lambda h: ref[0, pl.ds(k0, tk), lanes(h)]

    strictly_lower = (lax.broadcasted_iota(jnp.int32, (tq, tq), 1) < lax.broadcasted_iota(jnp.int32, (tq, tq), 0))
    own = weigh(scores(lambda h: kd_ref[0, :, lanes(h)]), lambda h: vd_ref[0, :, lanes(h)], trid_ref[...],
                valid=strictly_lower)
    for h, (pv, blocksum) in enumerate(own):
        acc_ref[h], carry_ref[h] = pv, blocksum

    @pl.loop(0, n_full)
    def _(t):
        kb = n_full - 1 - t
        carry = [carry_ref[h] for h in hs]
        for h, (pv, blocksum) in enumerate(
                weigh(scores(earlier(k_ref, kb)), earlier(v_ref, kb), tri_ref[...], carry=carry)):
            acc_ref[h] += pv
            carry_ref[h] = carry[h] + blocksum

    for h in range(heads):
        o_ref[0, :, lanes(h)] = (acc_ref[h] * _silu(z_ref[0, :, lanes(h)].astype(F32))).astype(o_ref.dtype)


def _suffix_sum_matrix(n):
    r = lax.broadcasted_iota(jnp.int32, (2 * n, n), 0) % n
    c = lax.broadcasted_iota(jnp.int32, (2 * n, n), 1)
    return (r >= c).astype(BF16)


def _stick_breaking(q, zg, k_new, v_new, past, tq, tk, heads):
    b, t, _ = q.shape
    tq = min(tq, t)
    w = heads * SB_HEAD_DIM
    assert t % tq == 0 and SB_HEADS % heads == 0
    if past is None:
        q_pos0, k_full, v_full = 0, k_new, v_new
        full_spec = pl.BlockSpec((1, t, w), lambda bi, h, i: (bi, 0, h))
    else:
        assert heads == SB_HEADS
        k_full, v_full, layer = past
        q_pos0 = k_full.shape[2] // SB_HEADS
        full_spec = pl.BlockSpec((None, 1, q_pos0 * SB_HEADS, SB_HEAD_DIM), lambda bi, h, i: (layer, bi, 0, 0))
    assert q_pos0 % tk == 0 and (t == tq or tq % tk == 0)
    tile = pl.BlockSpec((1, tq, w), lambda bi, h, i: (bi, i, h))
    kern = functools.partial(_sb_kernel, tq=tq, tk=tk, q_pos0=q_pos0, heads=heads, cache_rows=past is not None)
    return pl.pallas_call(
        kern,
        grid=(b, SB_HEADS // heads, t // tq),
        in_specs=[tile, tile, tile, full_spec, full_spec,
                  pl.BlockSpec((1, tq, w), lambda bi, h, i: (bi, i, BRANCH_W // w + h)),
                  pl.BlockSpec((2 * tq, tq), lambda bi, h, i: (0, 0)),
                  pl.BlockSpec((2 * tk, tk), lambda bi, h, i: (0, 0))],
        out_specs=tile,
        out_shape=jax.ShapeDtypeStruct((b, t, BRANCH_W), BF16),
        scratch_shapes=[pltpu.VMEM((heads, tq, SB_HEAD_DIM), F32), pltpu.VMEM((heads, tq, 1), F32)],
        compiler_params=_cparams("parallel", "parallel", "arbitrary"),
        name="stick_breaking",
    )(q, k_new, v_new, k_full, v_full, zg, _suffix_sum_matrix(tq), _suffix_sum_matrix(tk))


def _mlstm_kernel(q_ref, k_ref, v_ref, og_ref, z_ref, ifc_ref, ifr_ref, hn_ref, c0_ref, n0_ref, m0_ref,
                  y_ref, c_ref, n_ref, m_ref, *, chunk):
    ci = pl.program_id(1)

    @pl.when(ci == 0)
    def _():
        c_ref[...] = c0_ref[...]
        n_ref[...] = n0_ref[...]
        m_ref[...] = m0_ref[...]

    row = lax.broadcasted_iota(jnp.int32, (chunk, chunk), 0)
    col = lax.broadcasted_iota(jnp.int32, (chunk, chunk), 1)
    causal = col <= row
    log_k_scale = -0.5 * math.log(ML_HEAD_DIM)
    hs = range(ML_HEADS)
    sls = [slice(h * ML_HEAD_DIM, (h + 1) * ML_HEAD_DIM) for h in hs]
    q = [q_ref[0, :, sl] for sl in sls]
    k = [k_ref[0, :, sl] for sl in sls]
    v = [v_ref[0, :, sl] for sl in sls]
    c_prev = [c_ref[0, h] for h in hs]
    n_prev = [n_ref[0, h] for h in hs]
    m_prev = [m_ref[0, h, :, 0:1] for h in hs]
    qk = [lax.dot_general(q[h], k[h], _NT, preferred_element_type=F32) for h in hs]
    q_c = [jnp.dot(q[h], c_prev[h].astype(BF16), preferred_element_type=F32) for h in hs]

    f_col = [ifc_ref[0, :, ML_HEADS + h:ML_HEADS + h + 1] for h in hs]
    f_row = [ifr_ref[0, 0, ML_HEADS + h:ML_HEADS + h + 1, :] for h in hs]
    i_row = [ifr_ref[0, 0, h:h + 1, :] for h in hs]
    bcum_col = [jnp.sum(jnp.where(causal, f_row[h], 0.0), axis=1, keepdims=True) for h in hs]
    bcum_row = [jnp.sum(jnp.where(row <= col, f_col[h], 0.0), axis=0, keepdims=True) for h in hs]
    d = [jnp.where(causal, bcum_col[h] - bcum_row[h] + i_row[h], -jnp.inf) for h in hs]
    g = [bcum_col[h] + m_prev[h] for h in hs]
    m_t = [jnp.maximum(g[h], jnp.max(d[h], axis=1, keepdims=True)) for h in hs]
    w_inter = [jnp.exp(g[h] - m_t[h]) for h in hs]
    s = [qk[h] * jnp.exp(d[h] - (m_t[h] - log_k_scale)) for h in hs]
    sv = [jnp.dot(s[h].astype(BF16), v[h], preferred_element_type=F32) for h in hs]

    i_col = [ifc_ref[0, :, h:h + 1] for h in hs]
    b_last = [bcum_row[h][:, chunk - 1:chunk] for h in hs]
    m_new = [jnp.maximum(b_last[h] + m_prev[h],
                         jnp.max(b_last[h] - bcum_row[h] + i_row[h], axis=1, keepdims=True)) for h in hs]
    decay = [jnp.exp(b_last[h] + m_prev[h] - m_new[h]) for h in hs]
    w_end = [jnp.exp(b_last[h] - bcum_col[h] + i_col[h] - (m_new[h] - log_k_scale)) for h in hs]
    kw = [k[h].astype(F32) * w_end[h] for h in hs]
    kv = [lax.dot_general(kw[h].astype(BF16), v[h], (((0,), (0,)), ((), ())), preferred_element_type=F32)
          for h in hs]

    q_n = [jnp.sum(q[h].astype(F32) * n_prev[h], axis=1, keepdims=True) for h in hs]
    den = [jnp.sum(s[h], axis=1, keepdims=True) + w_inter[h] * q_n[h] for h in hs]
    h_out = [(sv[h] + w_inter[h] * q_c[h]) / jnp.maximum(jnp.abs(den[h]), jnp.exp(-m_t[h])) for h in hs]
    ms = [jnp.mean(h_out[h] * h_out[h], axis=1, keepdims=True) for h in hs]
    for h in hs:
        hn = h_out[h] * lax.rsqrt(ms[h] + RMS_EPS) * hn_ref[:, sls[h]]
        y = hn * _sigmoid(og_ref[0, :, sls[h]].astype(F32))
        y_ref[0, :, sls[h]] = (y * _silu(z_ref[0, :, sls[h]].astype(F32))).astype(y_ref.dtype)

    for h in hs:
        c_ref[0, h] = decay[h] * c_prev[h] + kv[h]
        n_ref[0, h] = decay[h] * n_prev[h] + jnp.sum(kw[h], axis=0, keepdims=True)
        m_ref[0, h] = jnp.broadcast_to(m_new[h], (1, LANES))


def _mlstm(ml, zg, ifc, hnorm_all, layer, c0, n0, m0, chunk):
    b, t, _ = ml.shape
    chunk = min(chunk, t)
    assert t % chunk == 0
    nc = t // chunk
    ifr = jnp.transpose(ifc[:, :, :2 * ML_HEADS].reshape(b, nc, chunk, 2 * ML_HEADS), (0, 1, 3, 2))
    kern = functools.partial(_mlstm_kernel, chunk=chunk)

    def colblock(cb):
        return pl.BlockSpec((1, chunk, BRANCH_W), lambda bi, ci: (bi, ci, cb))

    def state(shape):
        return pl.BlockSpec((1,) + shape, lambda bi, ci: (bi,) + (0,) * len(shape))

    c_shape = (ML_HEADS, ML_HEAD_DIM, ML_HEAD_DIM)
    n_shape = (ML_HEADS, 1, ML_HEAD_DIM)
    m_shape = (ML_HEADS, 1, LANES)
    return pl.pallas_call(
        kern,
        grid=(b, nc),
        in_specs=[colblock(0), colblock(1), colblock(2), colblock(3), colblock(2),
                  pl.BlockSpec((1, chunk, LANES), lambda bi, ci: (bi, ci, 0)),
                  pl.BlockSpec((1, 1, 2 * ML_HEADS, chunk), lambda bi, ci: (bi, ci, 0, 0)),
                  pl.BlockSpec((None, 1, BRANCH_W), lambda bi, ci: (layer, 0, 0)),
                  state(c_shape), state(n_shape), state(m_shape)],
        out_specs=[pl.BlockSpec((1, chunk, BRANCH_W), lambda bi, ci: (bi, ci, 0)),
                   state(c_shape), state(n_shape), state(m_shape)],
        out_shape=[jax.ShapeDtypeStruct((b, t, BRANCH_W), BF16),
                   jax.ShapeDtypeStruct((b,) + c_shape, F32),
                   jax.ShapeDtypeStruct((b,) + n_shape, F32),
                   jax.ShapeDtypeStruct((b,) + m_shape, F32)],
        compiler_params=_cparams("parallel", "arbitrary"),
        name="mlstm",
    )(ml, ml, ml, ml, zg, ifc, ifr, hnorm_all, c0, n0, m0)


def _merge_kernel(y0_ref, y1_ref, y2_ref, g0_ref, g1_ref, g2_ref, w_ref, o_ref):
    acc = None
    for bi, (y_ref, g_ref) in enumerate(((y0_ref, g0_ref), (y1_ref, g1_ref), (y2_ref, g2_ref))):
        term = _sigmoid(g_ref[...].astype(F32)) * jnp.dot(y_ref[...], w_ref[bi], preferred_element_type=F32)
        acc = term if acc is None else acc + term
    o_ref[...] = acc.astype(o_ref.dtype)


def _merge(ys, zg2d, w_branch_all, layer, tm=1024, tn=512):
    m = zg2d.shape[0]
    tm = min(tm, m)
    assert m % tm == 0 and D_MODEL % tn == 0 and ZG_GATE0 % tn == 0
    gb = ZG_GATE0 // tn

    def gate(bi):
        return pl.BlockSpec((tm, tn), lambda i, j: (i, gb + bi * (D_MODEL // tn) + j))

    y_spec = pl.BlockSpec((tm, BRANCH_W), lambda i, j: (i, 0))
    return pl.pallas_call(
        _merge_kernel,
        grid=(m // tm, D_MODEL // tn),
        in_specs=[y_spec, y_spec, y_spec, gate(0), gate(1), gate(2),
                  pl.BlockSpec((None, N_BRANCH, BRANCH_W, tn), lambda i, j: (layer, 0, 0, j))],
        out_specs=pl.BlockSpec((tm, tn), lambda i, j: (i, j)),
        out_shape=jax.ShapeDtypeStruct((m, D_MODEL), BF16),
        compiler_params=_cparams("parallel", "arbitrary"),
        name="merge",
    )(*ys, zg2d, zg2d, zg2d, w_branch_all)


def _out_kernel(a_ref, w_ref, x_ref, gpost_ref, gnext_ref, xo_ref, xn_ref):
    y = jnp.dot(a_ref[...], w_ref[...], preferred_element_type=F32)
    yn = y * lax.rsqrt(jnp.mean(y * y, axis=-1, keepdims=True) + RMS_EPS) * gpost_ref[...]
    xo = x_ref[...] + yn
    xo_ref[...] = xo
    xn = xo * lax.rsqrt(jnp.mean(xo * xo, axis=-1, keepdims=True) + RMS_EPS) * gnext_ref[...]
    xn_ref[...] = xn.astype(xn_ref.dtype)


def _out_proj(merged, w_out_all, x, g_post_all, g_pre_all, layer, tm=256):
    m = x.shape[0]
    tm = min(tm, m)
    assert m % tm == 0
    nxt = (layer + 1) % g_pre_all.shape[0]
    row = pl.BlockSpec((tm, D_MODEL), lambda i: (i, 0))
    return pl.pallas_call(
        _out_kernel,
        grid=(m // tm,),
        in_specs=[row,
                  pl.BlockSpec((None, D_MODEL, D_MODEL), lambda i: (layer, 0, 0)),
                  row,
                  pl.BlockSpec((None, 1, D_MODEL), lambda i: (layer, 0, 0)),
                  pl.BlockSpec((None, 1, D_MODEL), lambda i: (nxt, 0, 0))],
        out_specs=[row, row],
        out_shape=[jax.ShapeDtypeStruct((m, D_MODEL), F32), jax.ShapeDtypeStruct((m, D_MODEL), BF16)],
        compiler_params=_cparams("parallel"),
        name="out_proj",
    )(merged, w_out_all, x, g_post_all, g_pre_all)


def _trunk_layer(x, xn, shape, pos0, pool_prev16, kv_past, kv_new, c0, n0, m0, params, layer,
                 sb_tk=256, sb_tq=256, sb_heads=8, ml_chunk=256):
    b, t = shape
    wa, wz, w_if, b_if_pad, w_pool, pool_scale, hnorm, w_branch, w_out, g_post, g_pre = params
    (u,) = _mm(xn, wa, layer, OFF_POOL, BRANCH_W, (F32,))
    (q16,) = _mm(xn, wa, layer, OFF_SB_Q, BRANCH_W, (BF16,))
    k_all, k16 = _mm_heads(xn, wa, layer, OFF_SB_K, kv_new[0])
    v_all, v16 = _mm_heads(xn, wa, layer, OFF_SB_V, kv_new[1])
    (ml,) = _mm(xn, wa, layer, OFF_ML_Q, ML_COLS, (BF16,))
    (zg,) = _mm(xn, wz, layer, 0, ZG_COLS, (BF16,))
    ifc = _if_proj(xn, w_if, b_if_pad, layer)

    u3 = u.reshape(b, t, BRANCH_W)
    zg3 = zg.reshape(b, t, ZG_COLS)
    y_pool = _pool(u3, pool_prev16, zg3, w_pool, pool_scale, layer, pos0)

    past = None if kv_past is None else kv_past + (layer,)
    y_sb = _stick_breaking(q16.reshape(b, t, BRANCH_W), zg3, k16.reshape(b, t, BRANCH_W),
                           v16.reshape(b, t, BRANCH_W), past, sb_tq, sb_tk, sb_heads)

    y_ml, c_new, n_new, m_new = _mlstm(ml.reshape(b, t, ML_COLS), zg3, ifc.reshape(b, t, LANES), hnorm, layer,
                                       c0, n0, m0, ml_chunk)

    ys = [y.reshape(b * t, BRANCH_W) for y in (y_pool, y_sb, y_ml)]
    merged = _merge(ys, zg, w_branch, layer)
    x_out, xn_next = _out_proj(merged, w_out, x, g_post, g_pre, layer)

    pool_new = u3[:, t - POOL_STATE:, :]
    return x_out, xn_next, (k_all, v_all), (pool_new, c_new, n_new[:, :, 0, :], m_new[:, :, 0, 0])


def _cast_kernel(x_ref, o_ref):
    o_ref[...] = x_ref[...].astype(o_ref.dtype)


def _scaled_cast_kernel(x_ref, s_ref, o_ref):
    o_ref[...] = (x_ref[...] * s_ref[...]).astype(o_ref.dtype)


def _cast_bf16(x, col0, ncols, col_scale=None, tr=512, tc=2048):
    d, r, _ = x.shape
    tr = min(tr, r)
    assert r % tr == 0 and ncols % tc == 0 and col0 % tc == 0
    cb = col0 // tc
    blk = pl.BlockSpec((None, tr, tc), lambda i, j, k: (i, j, k))
    in_specs, args, kern = [pl.BlockSpec((None, tr, tc), lambda i, j, k: (i, j, cb + k))], [x], _cast_kernel
    if col_scale is not None:
        in_specs.append(pl.BlockSpec((1, tc), lambda i, j, k: (0, k)))
        args.append(col_scale)
        kern = _scaled_cast_kernel
    return pl.pallas_call(
        kern,
        grid=(d, r // tr, ncols // tc),
        in_specs=in_specs,
        out_specs=blk,
        out_shape=jax.ShapeDtypeStruct((d, r, ncols), BF16),
        compiler_params=_cparams("parallel", "parallel", "parallel"),
        name="cast_bf16",
    )(*args)


def _shifted_cast_kernel(x_ref, nxt_ref, o_ref, *, shift):
    x = jnp.concatenate([x_ref[:, shift:], nxt_ref[:, :shift]], axis=1)
    o_ref[...] = x.astype(o_ref.dtype)


def _shifted_cast_bf16(x, col0, shift, ncols, tr=512, tc=1024):
    d, r, c = x.shape
    assert r % tr == 0 and ncols % tc == 0 and col0 % tc == 0 and 0 < shift < LANES
    assert col0 + shift + ncols <= c
    cb, nb = col0 // tc, tc // LANES
    return pl.pallas_call(
        functools.partial(_shifted_cast_kernel, shift=shift),
        grid=(d, r // tr, ncols // tc),
        in_specs=[pl.BlockSpec((None, tr, tc), lambda i, j, k: (i, j, cb + k)),
                  pl.BlockSpec((None, tr, LANES), lambda i, j, k: (i, j, (cb + k + 1) * nb))],
        out_specs=pl.BlockSpec((None, tr, tc), lambda i, j, k: (i, j, k)),
        out_shape=jax.ShapeDtypeStruct((d, r, ncols), BF16),
        compiler_params=_cparams("parallel", "parallel", "parallel"),
        name="shifted_cast_bf16",
    )(x, x)


def _prep_params(g_pre, w_in, b_if, w_pool, pool_scale, ml_hnorm, w_branch, w_out, g_post):
    depth = w_in.shape[0]
    q_scale = SB_HEAD_DIM ** -0.5 * LOG2_E
    cols = lax.broadcasted_iota(jnp.int32, (1, OFF_ML_I), 1)
    col_scale = jnp.where((cols >= OFF_SB_Q) & (cols < OFF_SB_K), q_scale, 1.0).astype(F32)
    wa = _cast_bf16(w_in, 0, OFF_ML_I, col_scale)
    wz = _shifted_cast_bf16(w_in, OFF_ML_I, OFF_Z - OFF_ML_I, ZG_COLS)
    gate_cols = (lax.broadcasted_iota(jnp.int32, (1, LANES), 1) < 2 * ML_HEADS).astype(F32)
    w_if = _cast_bf16(w_in, OFF_ML_I, LANES, gate_cols, tc=LANES)
    b_if_pad = jnp.pad(b_if.reshape(depth, 1, 2 * ML_HEADS), ((0, 0), (0, 0), (0, LANES - 2 * ML_HEADS)))
    w_branch16 = _cast_bf16(w_branch.reshape(depth * N_BRANCH, BRANCH_W, D_MODEL), 0, D_MODEL)
    w_out16 = _cast_bf16(w_out, 0, D_MODEL)
    return (wa, wz, w_if, b_if_pad, w_pool.astype(BF16), pool_scale.reshape(depth, 1, BRANCH_W),
            ml_hnorm.reshape(depth, 1, BRANCH_W), w_branch16.reshape(depth, N_BRANCH, BRANCH_W, D_MODEL), w_out16,
            g_post.reshape(depth, 1, D_MODEL), g_pre.reshape(depth, 1, D_MODEL))


def kernel(x_prompt, x_sample, cache_sb_k, cache_sb_v, state_pool, state_ml_c, state_ml_n, state_ml_m,
           g_pre, w_in, b_if, w_pool, pool_scale, ml_hnorm, w_branch, w_out, g_post):
    bp, tp, _ = x_prompt.shape
    bs, ts, _ = x_sample.shape
    past_len = cache_sb_k.shape[2]
    params = _prep_params(g_pre, w_in, b_if, w_pool, pool_scale, ml_hnorm, w_branch, w_out, g_post)
    g_pre3 = params[-1]

    xp = x_prompt.reshape(bp * tp, D_MODEL)
    xs = x_sample.reshape(bs * ts, D_MODEL)
    xnp = _rmsnorm(xp, g_pre3, 0)
    xns = _rmsnorm(xs, g_pre3, 0)

    zero_pool = jnp.zeros((bp, POOL_HALO, BRANCH_W), F32)
    zero_c = jnp.zeros((bp, ML_HEADS, ML_HEAD_DIM, ML_HEAD_DIM), F32)
    zero_n = jnp.zeros((bp, ML_HEADS, 1, ML_HEAD_DIM), F32)
    zero_m = jnp.zeros((bp, ML_HEADS, 1, LANES), F32)
    pool_prev = jnp.pad(state_pool, ((0, 0), (0, 0), (POOL_HALO - POOL_STATE, 0), (0, 0)))
    n_prev = state_ml_n[:, :, :, None, :]
    m_prev = jnp.broadcast_to(state_ml_m[:, :, :, None, None], state_ml_m.shape + (1, LANES))

    kv_past = (cache_sb_k.reshape(DEPTH, bs, past_len * SB_HEADS, SB_HEAD_DIM),
               cache_sb_v.reshape(DEPTH, bs, past_len * SB_HEADS, SB_HEAD_DIM))

    kv_p = tuple(jnp.zeros((DEPTH, bp * tp * SB_HEADS, SB_HEAD_DIM), F32) for _ in range(2))
    kv_s = tuple(jnp.zeros((DEPTH, bs * ts * SB_HEADS, SB_HEAD_DIM), F32) for _ in range(2))

    pr, sa = [], []
    for l in range(DEPTH):
        xp, xnp, kv_p, outs_p = _trunk_layer(xp, xnp, (bp, tp), 0, zero_pool, None, kv_p,
                                             zero_c, zero_n, zero_m, params, l)
        xs, xns, kv_s, outs_s = _trunk_layer(xs, xns, (bs, ts), past_len, pool_prev[l], kv_past, kv_s,
                                             state_ml_c[l], n_prev[l], m_prev[l], params, l, sb_tq=ts)
        pr.append(outs_p)
        sa.append(outs_s)
    p_pool, p_c, p_n, p_m = [jnp.stack(a, axis=0) for a in zip(*pr)]
    s_pool, s_c, s_n, s_m = [jnp.stack(a, axis=0) for a in zip(*sa)]
    p_k, p_v = [a.reshape(DEPTH, bp, tp, SB_HEADS, SB_HEAD_DIM) for a in kv_p]
    s_k, s_v = [a.reshape(DEPTH, bs, ts, SB_HEADS, SB_HEAD_DIM) for a in kv_s]
    return (xp.reshape(bp, tp, D_MODEL), xs.reshape(bs, ts, D_MODEL),
            p_k, p_v, p_pool, p_c, p_n, p_m, s_k, s_v, s_pool, s_c, s_n, s_m)
```

```python
import functools
import math

import jax
import jax.numpy as jnp
from jax import lax
from jax.experimental import pallas as pl
from jax.experimental.pallas import tpu as pltpu

F32 = jnp.float32
BF16 = jnp.bfloat16

D_MODEL = 2048
DEPTH = 4
BRANCH_W = 1024
N_BRANCH = 3
POOL_WINDOWS = (2, 4, 8, 16)
POOL_GROUP_W = BRANCH_W // len(POOL_WINDOWS)
POOL_STATE = 15
POOL_HALO = 16
SB_HEADS = 8
SB_HEAD_DIM = BRANCH_W // SB_HEADS
ML_HEADS = 4
ML_HEAD_DIM = BRANCH_W // ML_HEADS
RMS_EPS = 1e-6
LANES = 128

OFF_POOL, OFF_SB_Q, OFF_SB_K, OFF_SB_V = 0, 1024, 2048, 3072
OFF_ML_Q, OFF_ML_K, OFF_ML_V, OFF_ML_O = 4096, 5120, 6144, 7168
OFF_ML_I = 8192
OFF_Z = OFF_ML_I + 2 * ML_HEADS
OFF_GATE = OFF_Z + N_BRANCH * BRANCH_W
IN_COLS = OFF_GATE + N_BRANCH * D_MODEL

ML_COLS = OFF_ML_I - OFF_ML_Q
ZG_COLS = IN_COLS - OFF_Z
ZG_GATE0 = N_BRANCH * BRANCH_W

VMEM_LIMIT = 56 * 1024 * 1024


def _cparams(*sem):
    return pltpu.CompilerParams(dimension_semantics=sem, vmem_limit_bytes=VMEM_LIMIT)


def _sigmoid(x):
    return 1.0 / (1.0 + jnp.exp(-x))


def _silu(x):
    return x * _sigmoid(x)


def _rmsnorm_kernel(x_ref, g_ref, o_ref):
    x = x_ref[...]
    ms = jnp.mean(x * x, axis=-1, keepdims=True)
    o_ref[...] = (x * lax.rsqrt(ms + RMS_EPS) * g_ref[...]).astype(o_ref.dtype)


def _rmsnorm(x, g_all, layer, tm=512):
    n, d = x.shape
    tm = min(tm, n)
    return pl.pallas_call(
        _rmsnorm_kernel,
        grid=(n // tm,),
        in_specs=[pl.BlockSpec((tm, d), lambda i: (i, 0)),
                  pl.BlockSpec((None, 1, d), lambda i: (layer, 0, 0))],
        out_specs=pl.BlockSpec((tm, d), lambda i: (i, 0)),
        out_shape=jax.ShapeDtypeStruct((n, d), BF16),
        compiler_params=_cparams("parallel"),
        name="rmsnorm",
    )(x, g_all)


def _mm_kernel(a_ref, b_ref, *o_refs):
    acc = jnp.dot(a_ref[...], b_ref[...], preferred_element_type=F32)
    for o_ref in o_refs:
        o_ref[...] = acc.astype(o_ref.dtype)


def _mm(a, w_all, layer, col0, ncols, out_dtypes, tm=1024, tn=1024):
    m, k = a.shape
    tm, tn = min(tm, m), min(tn, ncols)
    assert m % tm == 0 and ncols % tn == 0 and col0 % tn == 0
    cb = col0 // tn
    outs = [jax.ShapeDtypeStruct((m, ncols), dt) for dt in out_dtypes]
    res = pl.pallas_call(
        _mm_kernel,
        grid=(m // tm, ncols // tn),
        in_specs=[pl.BlockSpec((tm, k), lambda i, j: (i, 0)),
                  pl.BlockSpec((None, k, tn), lambda i, j: (layer, 0, j + cb))],
        out_specs=[pl.BlockSpec((tm, tn), lambda i, j: (i, j)) for _ in outs],
        out_shape=outs,
        compiler_params=_cparams("parallel", "arbitrary"),
        name="in_proj",
    )(a, w_all)
    return res


def _mm_heads_kernel(a_ref, b_ref, stacked_ref, o32_ref, o16_ref):
    del stacked_ref
    acc = jnp.dot(a_ref[...], b_ref[...], preferred_element_type=F32)
    tm = acc.shape[0]
    for h in range(SB_HEADS):
        o32_ref[pl.ds(h, tm, stride=SB_HEADS), :] = acc[:, h * SB_HEAD_DIM:(h + 1) * SB_HEAD_DIM]
    o16_ref[...] = acc.astype(o16_ref.dtype)


def _mm_heads(a, w_all, layer, col0, stacked, tm=1024):
    m, k = a.shape
    tm = min(tm, m)
    assert m % tm == 0 and col0 % BRANCH_W == 0 and stacked.shape[1:] == (m * SB_HEADS, SB_HEAD_DIM)
    cb = col0 // BRANCH_W
    return pl.pallas_call(
        _mm_heads_kernel,
        grid=(m // tm,),
        in_specs=[pl.BlockSpec((tm, k), lambda i: (i, 0)),
                  pl.BlockSpec((None, k, BRANCH_W), lambda i: (layer, 0, cb)),
                  pl.BlockSpec(memory_space=pl.ANY)],
        out_specs=[pl.BlockSpec((None, tm * SB_HEADS, SB_HEAD_DIM), lambda i: (layer, i, 0)),
                   pl.BlockSpec((tm, BRANCH_W), lambda i: (i, 0))],
        out_shape=[jax.ShapeDtypeStruct(stacked.shape, F32),
                   jax.ShapeDtypeStruct((m, BRANCH_W), BF16)],
        input_output_aliases={2: 0},
        compiler_params=_cparams("parallel"),
        name="in_proj_heads",
    )(a, w_all, stacked)


def _if_kernel(a_ref, w_ref, b_ref, o_ref):
    acc = jnp.dot(a_ref[...], w_ref[...], preferred_element_type=F32) + b_ref[...]
    col = lax.broadcasted_iota(jnp.int32, acc.shape, 1)
    log_sig = jnp.minimum(acc, 0.0) - jnp.log(1.0 + jnp.exp(-jnp.abs(acc)))
    o_ref[...] = jnp.where(col >= ML_HEADS, log_sig, acc)


def _if_proj(a, w_if, b_if_pad, layer, tm=1024):
    m, k = a.shape
    tm = min(tm, m)
    return pl.pallas_call(
        _if_kernel,
        grid=(m // tm,),
        in_specs=[pl.BlockSpec((tm, k), lambda i: (i, 0)),
                  pl.BlockSpec((None, k, LANES), lambda i: (layer, 0, 0)),
                  pl.BlockSpec((None, 1, LANES), lambda i: (layer, 0, 0))],
        out_specs=pl.BlockSpec((tm, LANES), lambda i: (i, 0)),
        out_shape=jax.ShapeDtypeStruct((m, LANES), F32),
        compiler_params=_cparams("parallel"),
        name="if_proj",
    )(a, w_if, b_if_pad)


def _pool_kernel(u_ref, halo_ref, prev_ref, z_ref, w_ref, s_ref, o_ref, full_ref, *, tt, pos0):
    t = pl.program_id(1)
    full_ref[0:POOL_HALO, :] = jnp.where(t == 0, prev_ref[0], halo_ref[0])
    u = u_ref[0]
    full_ref[POOL_HALO:POOL_HALO + tt, :] = u
    ys = []
    for gi, w in enumerate(POOL_WINDOWS):
        lo = gi * POOL_GROUP_W
        wsum = u[:, lo:lo + POOL_GROUP_W]
        for j in range(1, w):
            wsum = wsum + full_ref[POOL_HALO - j:POOL_HALO - j + tt, lo:lo + POOL_GROUP_W]
        if pos0 + 1 >= w:
            mean = wsum * (1.0 / w)
        else:
            pos = pos0 + t * tt + lax.broadcasted_iota(jnp.int32, (tt, 1), 0)
            mean = wsum / jnp.minimum(pos + 1, w).astype(F32)
        pooled = mean - u[:, lo:lo + POOL_GROUP_W]
        ys.append(jnp.dot(pooled.astype(BF16), w_ref[gi], preferred_element_type=F32))
    y = jnp.concatenate(ys, axis=-1) * s_ref[...]
    o_ref[0] = (y * _silu(z_ref[0].astype(F32))).astype(o_ref.dtype)


def _pool(u, prev16, zg, w_pool_all, scale_all, layer, pos0, tt=512):
    b, t, w = u.shape
    tt = min(tt, t)
    assert t % tt == 0 and tt % POOL_HALO == 0
    hb = tt // POOL_HALO
    kern = functools.partial(_pool_kernel, tt=tt, pos0=pos0)
    return pl.pallas_call(
        kern,
        grid=(b, t // tt),
        in_specs=[pl.BlockSpec((1, tt, w), lambda i, j: (i, j, 0)),
                  pl.BlockSpec((1, POOL_HALO, w), lambda i, j: (i, jnp.maximum(j * hb - 1, 0), 0)),
                  pl.BlockSpec((1, POOL_HALO, w), lambda i, j: (i, 0, 0)),
                  pl.BlockSpec((1, tt, w), lambda i, j: (i, j, 0)),
                  pl.BlockSpec((None, len(POOL_WINDOWS), POOL_GROUP_W, POOL_GROUP_W), lambda i, j: (layer, 0, 0, 0)),
                  pl.BlockSpec((None, 1, w), lambda i, j: (layer, 0, 0))],
        out_specs=pl.BlockSpec((1, tt, w), lambda i, j: (i, j, 0)),
        out_shape=jax.ShapeDtypeStruct((b, t, w), BF16),
        scratch_shapes=[pltpu.VMEM((POOL_HALO + tt, w), F32)],
        compiler_params=_cparams("parallel", "arbitrary"),
        name="pool",
    )(u, u, prev16, zg, w_pool_all, scale_all)


_NT = (((1,), (1,)), ((), ()))
_SIGN_BIT = -2 ** 31
LOG2_E = 1.4426950408889634


def _sb_kernel(q_ref, kd_ref, vd_ref, k_ref, v_ref, z_ref, trid_ref, tri_ref, o_ref, acc_ref, carry_ref, *,
               tq, tk, q_pos0, heads, cache_rows):
    i = pl.program_id(2)
    n_full = (q_pos0 + i * tq) // tk
    hs = range(heads)

    def lanes(h):
        return slice(h * SB_HEAD_DIM, (h + 1) * SB_HEAD_DIM)

    def block(k_of, v_of, tri, carry=None, valid=None):
        z2, suffix, total, out = {}, {}, {}, {}

        def score(h):
            z2[h] = lax.dot_general(q_ref[0, :, lanes(h)], k_of(h), _NT, preferred_element_type=F32)

        def suffix_sums(h):
            neg_abs = lax.bitcast_convert_type(lax.bitcast_convert_type(z2[h], jnp.int32) | _SIGN_BIT, F32)
            sp2 = jnp.maximum(z2[h], 0.0) + jnp.log(1.0 + jnp.exp2(neg_abs)) * LOG2_E
            if valid is not None:
                sp2 = jnp.where(valid, sp2, 0.0)
            suffix[h] = jnp.dot(sp2.astype(BF16), tri, preferred_element_type=F32)
            total[h] = jnp.sum(sp2, axis=1, keepdims=True)

        def weights(h):
            pre = z2[h] - suffix[h]
            if carry is not None:
                pre = pre - carry[h]
            if valid is not None:
                pre = jnp.where(valid, pre, -jnp.inf)
            out[h] = (jnp.dot(jnp.exp2(pre).astype(BF16), v_of(h), preferred_element_type=F32), total[h])

        for step in (score, suffix_sums, weights):
            for h in hs:
                step(h)
        return [out[h] for h in hs]

    def earlier(ref, kb):
        k0 = pl.multiple_of(kb * tk, tk)
        if cache_rows:
            base = pl.multiple_of(k0 * heads, tk * heads)
            return lambda h: ref[0, pl.ds(base + h, tk, stride=heads), :].astype(BF16)
        return lambda h: ref[0, pl.ds(k0, tk), lanes(h)]

    strictly_lower = (lax.broadcasted_iota(jnp.int32, (tq, tq), 1) < lax.broadcasted_iota(jnp.int32, (tq, tq), 0))
    own = block(lambda h: kd_ref[0, :, lanes(h)], lambda h: vd_ref[0, :, lanes(h)], trid_ref[...],
                valid=strictly_lower)
    for h, (pv, blocksum) in enumerate(own):
        acc_ref[h], carry_ref[h] = pv, blocksum

    @pl.loop(0, n_full)
    def _(t):
        kb = n_full - 1 - t
        carry = [carry_ref[h] for h in hs]
        for h, (pv, blocksum) in enumerate(
                block(earlier(k_ref, kb), earlier(v_ref, kb), tri_ref[...], carry=carry)):
            acc_ref[h] += pv
            carry_ref[h] = carry[h] + blocksum

    for h in range(heads):
        o_ref[0, :, lanes(h)] = (acc_ref[h] * _silu(z_ref[0, :, lanes(h)].astype(F32))).astype(o_ref.dtype)


def _suffix_sum_matrix(n):
    r = lax.broadcasted_iota(jnp.int32, (n, n), 0)
    c = lax.broadcasted_iota(jnp.int32, (n, n), 1)
    return (r >= c).astype(BF16)


def _stick_breaking(q, zg, k_new, v_new, past, tq, tk, heads):
    b, t, _ = q.shape
    tq = min(tq, t)
    w = heads * SB_HEAD_DIM
    assert t % tq == 0 and SB_HEADS % heads == 0
    if past is None:
        q_pos0, k_full, v_full = 0, k_new, v_new
        full_spec = pl.BlockSpec((1, t, w), lambda bi, h, i: (bi, 0, h))
    else:
        assert heads == SB_HEADS
        k_full, v_full, layer = past
        q_pos0 = k_full.shape[2] // SB_HEADS
        full_spec = pl.BlockSpec((None, 1, q_pos0 * SB_HEADS, SB_HEAD_DIM), lambda bi, h, i: (layer, bi, 0, 0))
    assert q_pos0 % tk == 0 and (t == tq or tq % tk == 0)
    tile = pl.BlockSpec((1, tq, w), lambda bi, h, i: (bi, i, h))
    kern = functools.partial(_sb_kernel, tq=tq, tk=tk, q_pos0=q_pos0, heads=heads, cache_rows=past is not None)
    return pl.pallas_call(
        kern,
        grid=(b, SB_HEADS // heads, t // tq),
        in_specs=[tile, tile, tile, full_spec, full_spec,
                  pl.BlockSpec((1, tq, w), lambda bi, h, i: (bi, i, BRANCH_W // w + h)),
                  pl.BlockSpec((tq, tq), lambda bi, h, i: (0, 0)),
                  pl.BlockSpec((tk, tk), lambda bi, h, i: (0, 0))],
        out_specs=tile,
        out_shape=jax.ShapeDtypeStruct((b, t, BRANCH_W), BF16),
        scratch_shapes=[pltpu.VMEM((heads, tq, SB_HEAD_DIM), F32), pltpu.VMEM((heads, tq, 1), F32)],
        compiler_params=_cparams("parallel", "parallel", "arbitrary"),
        name="stick_breaking",
    )(q, k_new, v_new, k_full, v_full, zg, _suffix_sum_matrix(tq), _suffix_sum_matrix(tk))


def _mlstm_kernel(q_ref, k_ref, v_ref, og_ref, z_ref, ifc_ref, ifr_ref, hn_ref, c0_ref, n0_ref, m0_ref,
                  y_ref, c_ref, n_ref, m_ref, *, chunk):
    ci = pl.program_id(1)

    @pl.when(ci == 0)
    def _():
        c_ref[...] = c0_ref[...]
        n_ref[...] = n0_ref[...]
        m_ref[...] = m0_ref[...]

    row = lax.broadcasted_iota(jnp.int32, (chunk, chunk), 0)
    col = lax.broadcasted_iota(jnp.int32, (chunk, chunk), 1)
    causal = col <= row
    log_k_scale = -0.5 * math.log(ML_HEAD_DIM)
    hs = range(q_ref.shape[0] * ML_HEADS)
    bi = [h // ML_HEADS for h in hs]
    hd = [h % ML_HEADS for h in hs]
    sls = [slice(hd[h] * ML_HEAD_DIM, (hd[h] + 1) * ML_HEAD_DIM) for h in hs]
    q = [q_ref[bi[h], :, sls[h]] for h in hs]
    k = [k_ref[bi[h], :, sls[h]] for h in hs]
    v = [v_ref[bi[h], :, sls[h]] for h in hs]
    c_prev = [c_ref[bi[h], hd[h]] for h in hs]
    n_prev = [n_ref[bi[h], hd[h]] for h in hs]
    m_prev = [m_ref[bi[h], hd[h], :, 0:1] for h in hs]
    qk = [lax.dot_general(q[h], k[h], _NT, preferred_element_type=F32) for h in hs]
    q_c = [jnp.dot(q[h], c_prev[h].astype(BF16), preferred_element_type=F32) for h in hs]

    f_col = [ifc_ref[bi[h], :, ML_HEADS + hd[h]:ML_HEADS + hd[h] + 1] for h in hs]
    f_row = [ifr_ref[bi[h], 0, ML_HEADS + hd[h]:ML_HEADS + hd[h] + 1, :] for h in hs]
    i_row = [ifr_ref[bi[h], 0, hd[h]:hd[h] + 1, :] for h in hs]
    bcum_col = [jnp.sum(jnp.where(causal, f_row[h], 0.0), axis=1, keepdims=True) for h in hs]
    bcum_row = [jnp.sum(jnp.where(row <= col, f_col[h], 0.0), axis=0, keepdims=True) for h in hs]
    d = [jnp.where(causal, bcum_col[h] - bcum_row[h] + i_row[h], -jnp.inf) for h in hs]
    g = [bcum_col[h] + m_prev[h] for h in hs]
    m_t = [jnp.maximum(g[h], jnp.max(d[h], axis=1, keepdims=True)) for h in hs]
    w_inter = [jnp.exp(g[h] - m_t[h]) for h in hs]
    s = [qk[h] * jnp.exp(d[h] - (m_t[h] - log_k_scale)) for h in hs]
    sv = [jnp.dot(s[h].astype(BF16), v[h], preferred_element_type=F32) for h in hs]

    i_col = [ifc_ref[bi[h], :, hd[h]:hd[h] + 1] for h in hs]
    b_last = [bcum_row[h][:, chunk - 1:chunk] for h in hs]
    m_new = [jnp.maximum(b_last[h] + m_prev[h],
                         jnp.max(b_last[h] - bcum_row[h] + i_row[h], axis=1, keepdims=True)) for h in hs]
    decay = [jnp.exp(b_last[h] + m_prev[h] - m_new[h]) for h in hs]
    w_end = [jnp.exp(b_last[h] - bcum_col[h] + i_col[h] - (m_new[h] - log_k_scale)) for h in hs]
    kw = [k[h].astype(F32) * w_end[h] for h in hs]
    kv = [lax.dot_general(kw[h].astype(BF16), v[h], (((0,), (0,)), ((), ())), preferred_element_type=F32)
          for h in hs]

    q_n = [jnp.sum(q[h].astype(F32) * n_prev[h], axis=1, keepdims=True) for h in hs]
    den = [jnp.sum(s[h], axis=1, keepdims=True) + w_inter[h] * q_n[h] for h in hs]
    h_out = [(sv[h] + w_inter[h] * q_c[h]) / jnp.maximum(jnp.abs(den[h]), jnp.exp(-m_t[h])) for h in hs]
    ms = [jnp.mean(h_out[h] * h_out[h], axis=1, keepdims=True) for h in hs]
    for h in hs:
        hn = h_out[h] * lax.rsqrt(ms[h] + RMS_EPS) * hn_ref[:, sls[h]]
        y = hn * _sigmoid(og_ref[bi[h], :, sls[h]].astype(F32))
        y_ref[bi[h], :, sls[h]] = (y * _silu(z_ref[bi[h], :, sls[h]].astype(F32))).astype(y_ref.dtype)

    for h in hs:
        c_ref[bi[h], hd[h]] = decay[h] * c_prev[h] + kv[h]
        n_ref[bi[h], hd[h]] = decay[h] * n_prev[h] + jnp.sum(kw[h], axis=0, keepdims=True)
        m_ref[bi[h], hd[h]] = jnp.broadcast_to(m_new[h], (1, LANES))


def _mlstm(ml, zg, ifc, hnorm_all, layer, c0, n0, m0, chunk, nb=1):
    b, t, _ = ml.shape
    chunk = min(chunk, t)
    assert t % chunk == 0 and b % nb == 0
    nc = t // chunk
    ifr = jnp.transpose(ifc[:, :, :2 * ML_HEADS].reshape(b, nc, chunk, 2 * ML_HEADS), (0, 1, 3, 2))
    kern = functools.partial(_mlstm_kernel, chunk=chunk)

    def colblock(cb):
        return pl.BlockSpec((nb, chunk, BRANCH_W), lambda bi, ci: (bi, ci, cb))

    def state(shape):
        return pl.BlockSpec((nb,) + shape, lambda bi, ci: (bi,) + (0,) * len(shape))

    c_shape = (ML_HEADS, ML_HEAD_DIM, ML_HEAD_DIM)
    n_shape = (ML_HEADS, 1, ML_HEAD_DIM)
    m_shape = (ML_HEADS, 1, LANES)
    return pl.pallas_call(
        kern,
        grid=(b // nb, nc),
        in_specs=[colblock(0), colblock(1), colblock(2), colblock(3), colblock(2),
                  pl.BlockSpec((nb, chunk, LANES), lambda bi, ci: (bi, ci, 0)),
                  pl.BlockSpec((nb, 1, 2 * ML_HEADS, chunk), lambda bi, ci: (bi, ci, 0, 0)),
                  pl.BlockSpec((None, 1, BRANCH_W), lambda bi, ci: (layer, 0, 0)),
                  state(c_shape), state(n_shape), state(m_shape)],
        out_specs=[pl.BlockSpec((nb, chunk, BRANCH_W), lambda bi, ci: (bi, ci, 0)),
                   state(c_shape), state(n_shape), state(m_shape)],
        out_shape=[jax.ShapeDtypeStruct((b, t, BRANCH_W), BF16),
                   jax.ShapeDtypeStruct((b,) + c_shape, F32),
                   jax.ShapeDtypeStruct((b,) + n_shape, F32),
                   jax.ShapeDtypeStruct((b,) + m_shape, F32)],
        compiler_params=_cparams("parallel", "arbitrary"),
        name="mlstm",
    )(ml, ml, ml, ml, zg, ifc, ifr, hnorm_all, c0, n0, m0)


def _merge_kernel(y0_ref, y1_ref, y2_ref, g0_ref, g1_ref, g2_ref, w_ref, o_ref):
    acc = None
    for bi, (y_ref, g_ref) in enumerate(((y0_ref, g0_ref), (y1_ref, g1_ref), (y2_ref, g2_ref))):
        term = _sigmoid(g_ref[...].astype(F32)) * jnp.dot(y_ref[...], w_ref[bi], preferred_element_type=F32)
        acc = term if acc is None else acc + term
    o_ref[...] = acc.astype(o_ref.dtype)


def _merge(ys, zg2d, w_branch_all, layer, tm=1024, tn=1024):
    m = zg2d.shape[0]
    tm = min(tm, m)
    assert m % tm == 0 and D_MODEL % tn == 0 and ZG_GATE0 % tn == 0
    gb = ZG_GATE0 // tn

    def gate(bi):
        return pl.BlockSpec((tm, tn), lambda i, j: (i, gb + bi * (D_MODEL // tn) + j))

    y_spec = pl.BlockSpec((tm, BRANCH_W), lambda i, j: (i, 0))
    return pl.pallas_call(
        _merge_kernel,
        grid=(m // tm, D_MODEL // tn),
        in_specs=[y_spec, y_spec, y_spec, gate(0), gate(1), gate(2),
                  pl.BlockSpec((None, N_BRANCH, BRANCH_W, tn), lambda i, j: (layer, 0, 0, j))],
        out_specs=pl.BlockSpec((tm, tn), lambda i, j: (i, j)),
        out_shape=jax.ShapeDtypeStruct((m, D_MODEL), BF16),
        compiler_params=_cparams("parallel", "arbitrary"),
        name="merge",
    )(*ys, zg2d, zg2d, zg2d, w_branch_all)


def _out_kernel(a_ref, w_ref, x_ref, gpost_ref, gnext_ref, xo_ref, xn_ref):
    y = jnp.dot(a_ref[...], w_ref[...], preferred_element_type=F32)
    yn = y * lax.rsqrt(jnp.mean(y * y, axis=-1, keepdims=True) + RMS_EPS) * gpost_ref[...]
    xo = x_ref[...] + yn
    xo_ref[...] = xo
    xn = xo * lax.rsqrt(jnp.mean(xo * xo, axis=-1, keepdims=True) + RMS_EPS) * gnext_ref[...]
    xn_ref[...] = xn.astype(xn_ref.dtype)


def _out_proj(merged, w_out_all, x, g_post_all, g_pre_all, layer, tm=512):
    m = x.shape[0]
    tm = min(tm, m)
    assert m % tm == 0
    nxt = (layer + 1) % g_pre_all.shape[0]
    row = pl.BlockSpec((tm, D_MODEL), lambda i: (i, 0))
    return pl.pallas_call(
        _out_kernel,
        grid=(m // tm,),
        in_specs=[row,
                  pl.BlockSpec((None, D_MODEL, D_MODEL), lambda i: (layer, 0, 0)),
                  row,
                  pl.BlockSpec((None, 1, D_MODEL), lambda i: (layer, 0, 0)),
                  pl.BlockSpec((None, 1, D_MODEL), lambda i: (nxt, 0, 0))],
        out_specs=[row, row],
        out_shape=[jax.ShapeDtypeStruct((m, D_MODEL), F32), jax.ShapeDtypeStruct((m, D_MODEL), BF16)],
        compiler_params=_cparams("parallel"),
        name="out_proj",
    )(merged, w_out_all, x, g_post_all, g_pre_all)


def _trunk_layer(x, xn, shape, pos0, pool_prev16, kv_past, kv_new, c0, n0, m0, params, layer,
                 sb_tk=256, sb_tq=256, sb_heads=8, ml_chunk=256):
    b, t = shape
    wa, wz, w_if, b_if_pad, w_pool, pool_scale, hnorm, w_branch, w_out, g_post, g_pre = params
    (u,) = _mm(xn, wa, layer, OFF_POOL, BRANCH_W, (F32,))
    (q16,) = _mm(xn, wa, layer, OFF_SB_Q, BRANCH_W, (BF16,))
    k_all, k16 = _mm_heads(xn, wa, layer, OFF_SB_K, kv_new[0])
    v_all, v16 = _mm_heads(xn, wa, layer, OFF_SB_V, kv_new[1])
    (ml,) = _mm(xn, wa, layer, OFF_ML_Q, ML_COLS, (BF16,))
    (zg,) = _mm(xn, wz, layer, 0, ZG_COLS, (BF16,))
    ifc = _if_proj(xn, w_if, b_if_pad, layer)

    u3 = u.reshape(b, t, BRANCH_W)
    zg3 = zg.reshape(b, t, ZG_COLS)
    y_pool = _pool(u3, pool_prev16, zg3, w_pool, pool_scale, layer, pos0)

    past = None if kv_past is None else kv_past + (layer,)
    y_sb = _stick_breaking(q16.reshape(b, t, BRANCH_W), zg3, k16.reshape(b, t, BRANCH_W),
                           v16.reshape(b, t, BRANCH_W), past, sb_tq, sb_tk, sb_heads)

    y_ml, c_new, n_new, m_new = _mlstm(ml.reshape(b, t, ML_COLS), zg3, ifc.reshape(b, t, LANES), hnorm, layer,
                                       c0, n0, m0, ml_chunk)

    ys = [y.reshape(b * t, BRANCH_W) for y in (y_pool, y_sb, y_ml)]
    merged = _merge(ys, zg, w_branch, layer)
    x_out, xn_next = _out_proj(merged, w_out, x, g_post, g_pre, layer)

    pool_new = u3[:, t - POOL_STATE:, :]
    return x_out, xn_next, (k_all, v_all), (pool_new, c_new, n_new[:, :, 0, :], m_new[:, :, 0, 0])


def _cast_kernel(x_ref, o_ref):
    o_ref[...] = x_ref[...].astype(o_ref.dtype)


def _scaled_cast_kernel(x_ref, s_ref, o_ref):
    o_ref[...] = (x_ref[...] * s_ref[...]).astype(o_ref.dtype)


def _cast_bf16(x, col0, ncols, col_scale=None, tr=512, tc=2048):
    d, r, _ = x.shape
    tr = min(tr, r)
    assert r % tr == 0 and ncols % tc == 0 and col0 % tc == 0
    cb = col0 // tc
    blk = pl.BlockSpec((None, tr, tc), lambda i, j, k: (i, j, k))
    in_specs, args, kern = [pl.BlockSpec((None, tr, tc), lambda i, j, k: (i, j, cb + k))], [x], _cast_kernel
    if col_scale is not None:
        in_specs.append(pl.BlockSpec((1, tc), lambda i, j, k: (0, k)))
        args.append(col_scale)
        kern = _scaled_cast_kernel
    return pl.pallas_call(
        kern,
        grid=(d, r // tr, ncols // tc),
        in_specs=in_specs,
        out_specs=blk,
        out_shape=jax.ShapeDtypeStruct((d, r, ncols), BF16),
        compiler_params=_cparams("parallel", "parallel", "parallel"),
        name="cast_bf16",
    )(*args)


def _shifted_cast_kernel(x_ref, nxt_ref, o_ref, *, shift):
    x = jnp.concatenate([x_ref[:, shift:], nxt_ref[:, :shift]], axis=1)
    o_ref[...] = x.astype(o_ref.dtype)


def _shifted_cast_bf16(x, col0, shift, ncols, tr=512, tc=1024):
    d, r, c = x.shape
    assert r % tr == 0 and ncols % tc == 0 and col0 % tc == 0 and 0 < shift < LANES
    assert col0 + shift + ncols <= c
    cb, nb = col0 // tc, tc // LANES
    return pl.pallas_call(
        functools.partial(_shifted_cast_kernel, shift=shift),
        grid=(d, r // tr, ncols // tc),
        in_specs=[pl.BlockSpec((None, tr, tc), lambda i, j, k: (i, j, cb + k)),
                  pl.BlockSpec((None, tr, LANES), lambda i, j, k: (i, j, (cb + k + 1) * nb))],
        out_specs=pl.BlockSpec((None, tr, tc), lambda i, j, k: (i, j, k)),
        out_shape=jax.ShapeDtypeStruct((d, r, ncols), BF16),
        compiler_params=_cparams("parallel", "parallel", "parallel"),
        name="shifted_cast_bf16",
    )(x, x)


def _prep_params(g_pre, w_in, b_if, w_pool, pool_scale, ml_hnorm, w_branch, w_out, g_post):
    depth = w_in.shape[0]
    q_scale = SB_HEAD_DIM ** -0.5 * LOG2_E
    cols = lax.broadcasted_iota(jnp.int32, (1, OFF_ML_I), 1)
    col_scale = jnp.where((cols >= OFF_SB_Q) & (cols < OFF_SB_K), q_scale, 1.0).astype(F32)
    wa = _cast_bf16(w_in, 0, OFF_ML_I, col_scale)
    wz = _shifted_cast_bf16(w_in, OFF_ML_I, OFF_Z - OFF_ML_I, ZG_COLS)
    gate_cols = (lax.broadcasted_iota(jnp.int32, (1, LANES), 1) < 2 * ML_HEADS).astype(F32)
    w_if = _cast_bf16(w_in, OFF_ML_I, LANES, gate_cols, tc=LANES)
    b_if_pad = jnp.pad(b_if.reshape(depth, 1, 2 * ML_HEADS), ((0, 0), (0, 0), (0, LANES - 2 * ML_HEADS)))
    w_branch16 = _cast_bf16(w_branch.reshape(depth * N_BRANCH, BRANCH_W, D_MODEL), 0, D_MODEL)
    w_out16 = _cast_bf16(w_out, 0, D_MODEL)
    return (wa, wz, w_if, b_if_pad, w_pool.astype(BF16), pool_scale.reshape(depth, 1, BRANCH_W),
            ml_hnorm.reshape(depth, 1, BRANCH_W), w_branch16.reshape(depth, N_BRANCH, BRANCH_W, D_MODEL), w_out16,
            g_post.reshape(depth, 1, D_MODEL), g_pre.reshape(depth, 1, D_MODEL))


def kernel(x_prompt, x_sample, cache_sb_k, cache_sb_v, state_pool, state_ml_c, state_ml_n, state_ml_m,
           g_pre, w_in, b_if, w_pool, pool_scale, ml_hnorm, w_branch, w_out, g_post):
    bp, tp, _ = x_prompt.shape
    bs, ts, _ = x_sample.shape
    past_len = cache_sb_k.shape[2]
    params = _prep_params(g_pre, w_in, b_if, w_pool, pool_scale, ml_hnorm, w_branch, w_out, g_post)
    g_pre3 = params[-1]

    xp = x_prompt.reshape(bp * tp, D_MODEL)
    xs = x_sample.reshape(bs * ts, D_MODEL)
    xnp = _rmsnorm(xp, g_pre3, 0)
    xns = _rmsnorm(xs, g_pre3, 0)

    zero_pool = jnp.zeros((bp, POOL_HALO, BRANCH_W), F32)
    zero_c = jnp.zeros((bp, ML_HEADS, ML_HEAD_DIM, ML_HEAD_DIM), F32)
    zero_n = jnp.zeros((bp, ML_HEADS, 1, ML_HEAD_DIM), F32)
    zero_m = jnp.zeros((bp, ML_HEADS, 1, LANES), F32)
    pool_prev = jnp.pad(state_pool, ((0, 0), (0, 0), (POOL_HALO - POOL_STATE, 0), (0, 0)))
    n_prev = state_ml_n[:, :, :, None, :]
    m_prev = jnp.broadcast_to(state_ml_m[:, :, :, None, None], state_ml_m.shape + (1, LANES))

    kv_past = (cache_sb_k.reshape(DEPTH, bs, past_len * SB_HEADS, SB_HEAD_DIM),
               cache_sb_v.reshape(DEPTH, bs, past_len * SB_HEADS, SB_HEAD_DIM))

    kv_p = tuple(jnp.zeros((DEPTH, bp * tp * SB_HEADS, SB_HEAD_DIM), F32) for _ in range(2))
    kv_s = tuple(jnp.zeros((DEPTH, bs * ts * SB_HEADS, SB_HEAD_DIM), F32) for _ in range(2))

    pr, sa = [], []
    for l in range(DEPTH):
        xp, xnp, kv_p, outs_p = _trunk_layer(xp, xnp, (bp, tp), 0, zero_pool, None, kv_p,
                                             zero_c, zero_n, zero_m, params, l)
        xs, xns, kv_s, outs_s = _trunk_layer(xs, xns, (bs, ts), past_len, pool_prev[l], kv_past, kv_s,
                                             state_ml_c[l], n_prev[l], m_prev[l], params, l, sb_tq=ts)
        pr.append(outs_p)
        sa.append(outs_s)
    p_pool, p_c, p_n, p_m = [jnp.stack(a, axis=0) for a in zip(*pr)]
    s_pool, s_c, s_n, s_m = [jnp.stack(a, axis=0) for a in zip(*sa)]
    p_k, p_v = [a.reshape(DEPTH, bp, tp, SB_HEADS, SB_HEAD_DIM) for a in kv_p]
    s_k, s_v = [a.reshape(DEPTH, bs, ts, SB_HEADS, SB_HEAD_DIM) for a in kv_s]
    return (xp.reshape(bp, tp, D_MODEL), xs.reshape(bs, ts, D_MODEL),
            p_k, p_v, p_pool, p_c, p_n, p_m, s_k, s_v, s_pool, s_c, s_n, s_m)
```

```python
import functools
import math

import jax
import jax.numpy as jnp
from jax import lax
from jax.experimental import pallas as pl
from jax.experimental.pallas import tpu as pltpu

F32 = jnp.float32
BF16 = jnp.bfloat16

D_MODEL = 2048
DEPTH = 4
BRANCH_W = 1024
N_BRANCH = 3
POOL_WINDOWS = (2, 4, 8, 16)
POOL_GROUP_W = BRANCH_W // len(POOL_WINDOWS)
POOL_STATE = 15
POOL_HALO = 16
SB_HEADS = 8
SB_HEAD_DIM = BRANCH_W // SB_HEADS
ML_HEADS = 4
ML_HEAD_DIM = BRANCH_W // ML_HEADS
RMS_EPS = 1e-6
LANES = 128

OFF_POOL, OFF_SB_Q, OFF_SB_K, OFF_SB_V = 0, 1024, 2048, 3072
OFF_ML_Q, OFF_ML_K, OFF_ML_V, OFF_ML_O = 4096, 5120, 6144, 7168
OFF_ML_I = 8192
OFF_Z = OFF_ML_I + 2 * ML_HEADS
OFF_GATE = OFF_Z + N_BRANCH * BRANCH_W
IN_COLS = OFF_GATE + N_BRANCH * D_MODEL

ML_COLS = OFF_ML_I - OFF_ML_Q
ZG_COLS = IN_COLS - OFF_Z
ZG_GATE0 = N_BRANCH * BRANCH_W

VMEM_LIMIT = 56 * 1024 * 1024


def _cparams(*sem):
    return pltpu.CompilerParams(dimension_semantics=sem, vmem_limit_bytes=VMEM_LIMIT)


def _sigmoid(x):
    return 1.0 / (1.0 + jnp.exp(-x))


def _silu(x):
    return x * _sigmoid(x)


def _rmsnorm_kernel(x_ref, g_ref, o_ref):
    x = x_ref[...]
    ms = jnp.mean(x * x, axis=-1, keepdims=True)
    o_ref[...] = (x * lax.rsqrt(ms + RMS_EPS) * g_ref[...]).astype(o_ref.dtype)


def _rmsnorm(x, g_all, layer, tm=512):
    n, d = x.shape
    tm = min(tm, n)
    return pl.pallas_call(
        _rmsnorm_kernel,
        grid=(n // tm,),
        in_specs=[pl.BlockSpec((tm, d), lambda i: (i, 0)),
                  pl.BlockSpec((None, 1, d), lambda i: (layer, 0, 0))],
        out_specs=pl.BlockSpec((tm, d), lambda i: (i, 0)),
        out_shape=jax.ShapeDtypeStruct((n, d), BF16),
        compiler_params=_cparams("parallel"),
        name="rmsnorm",
    )(x, g_all)


def _mm_kernel(a_ref, b_ref, *o_refs):
    acc = jnp.dot(a_ref[...], b_ref[...], preferred_element_type=F32)
    for o_ref in o_refs:
        o_ref[...] = acc.astype(o_ref.dtype)


def _mm(a, w_all, layer, col0, ncols, out_dtypes, tm=1024, tn=1024):
    m, k = a.shape
    tm, tn = min(tm, m), min(tn, ncols)
    assert m % tm == 0 and ncols % tn == 0 and col0 % tn == 0
    cb = col0 // tn
    outs = [jax.ShapeDtypeStruct((m, ncols), dt) for dt in out_dtypes]
    res = pl.pallas_call(
        _mm_kernel,
        grid=(m // tm, ncols // tn),
        in_specs=[pl.BlockSpec((tm, k), lambda i, j: (i, 0)),
                  pl.BlockSpec((None, k, tn), lambda i, j: (layer, 0, j + cb))],
        out_specs=[pl.BlockSpec((tm, tn), lambda i, j: (i, j)) for _ in outs],
        out_shape=outs,
        compiler_params=_cparams("parallel", "arbitrary"),
        name="in_proj",
    )(a, w_all)
    return res


def _mm_heads_kernel(a_ref, b_ref, stacked_ref, o32_ref, o16_ref):
    del stacked_ref
    acc = jnp.dot(a_ref[...], b_ref[...], preferred_element_type=F32)
    tm = acc.shape[0]
    for h in range(SB_HEADS):
        o32_ref[pl.ds(h, tm, stride=SB_HEADS), :] = acc[:, h * SB_HEAD_DIM:(h + 1) * SB_HEAD_DIM]
    o16_ref[...] = acc.astype(o16_ref.dtype)


def _mm_heads(a, w_all, layer, col0, stacked, tm=1024):
    m, k = a.shape
    tm = min(tm, m)
    assert m % tm == 0 and col0 % BRANCH_W == 0 and stacked.shape[1:] == (m * SB_HEADS, SB_HEAD_DIM)
    cb = col0 // BRANCH_W
    return pl.pallas_call(
        _mm_heads_kernel,
        grid=(m // tm,),
        in_specs=[pl.BlockSpec((tm, k), lambda i: (i, 0)),
                  pl.BlockSpec((None, k, BRANCH_W), lambda i: (layer, 0, cb)),
                  pl.BlockSpec(memory_space=pl.ANY)],
        out_specs=[pl.BlockSpec((None, tm * SB_HEADS, SB_HEAD_DIM), lambda i: (layer, i, 0)),
                   pl.BlockSpec((tm, BRANCH_W), lambda i: (i, 0))],
        out_shape=[jax.ShapeDtypeStruct(stacked.shape, F32),
                   jax.ShapeDtypeStruct((m, BRANCH_W), BF16)],
        input_output_aliases={2: 0},
        compiler_params=_cparams("parallel"),
        name="in_proj_heads",
    )(a, w_all, stacked)


def _pool_if_kernel(a_ref, wu_ref, wif_ref, b_ref, u_ref, if_ref):
    a = a_ref[...]
    u_ref[...] = jnp.dot(a, wu_ref[...], preferred_element_type=F32)
    acc = jnp.dot(a, wif_ref[...], preferred_element_type=F32) + b_ref[...]
    col = lax.broadcasted_iota(jnp.int32, acc.shape, 1)
    log_sig = jnp.minimum(acc, 0.0) - jnp.log(1.0 + jnp.exp(-jnp.abs(acc)))
    if_ref[...] = jnp.where(col >= ML_HEADS, log_sig, acc)


def _pool_if_proj(a, wa, w_if, b_if_pad, layer, tm=1024):
    m, k = a.shape
    tm = min(tm, m)
    assert m % tm == 0
    return pl.pallas_call(
        _pool_if_kernel,
        grid=(m // tm,),
        in_specs=[pl.BlockSpec((tm, k), lambda i: (i, 0)),
                  pl.BlockSpec((None, k, BRANCH_W), lambda i: (layer, 0, OFF_POOL // BRANCH_W)),
                  pl.BlockSpec((None, k, LANES), lambda i: (layer, 0, 0)),
                  pl.BlockSpec((None, 1, LANES), lambda i: (layer, 0, 0))],
        out_specs=[pl.BlockSpec((tm, BRANCH_W), lambda i: (i, 0)),
                   pl.BlockSpec((tm, LANES), lambda i: (i, 0))],
        out_shape=[jax.ShapeDtypeStruct((m, BRANCH_W), F32), jax.ShapeDtypeStruct((m, LANES), F32)],
        compiler_params=_cparams("parallel"),
        name="pool_if_proj",
    )(a, wa, w_if, b_if_pad)


def _pool_kernel(u_ref, halo_ref, prev_ref, z_ref, w_ref, s_ref, o_ref, *, tt, pos0):
    t = pl.program_id(1)
    halo = jnp.where(t == 0, prev_ref[0], halo_ref[0])
    u = u_ref[0]
    ys = []
    for gi, w in enumerate(POOL_WINDOWS):
        lo = gi * POOL_GROUP_W
        x = jnp.concatenate([halo[:, lo:lo + POOL_GROUP_W], u[:, lo:lo + POOL_GROUP_W]], axis=0)
        s = 1
        while s < w:
            x = x + pltpu.roll(x, s, axis=0)
            s *= 2
        wsum = x[POOL_HALO:, :]
        if pos0 + 1 >= w:
            mean = wsum * (1.0 / w)
        else:
            pos = pos0 + t * tt + lax.broadcasted_iota(jnp.int32, (tt, 1), 0)
            mean = wsum / jnp.minimum(pos + 1, w).astype(F32)
        pooled = mean - u[:, lo:lo + POOL_GROUP_W]
        ys.append(jnp.dot(pooled.astype(BF16), w_ref[gi], preferred_element_type=F32))
    y = jnp.concatenate(ys, axis=-1) * s_ref[...]
    o_ref[0] = (y * _silu(z_ref[0].astype(F32))).astype(o_ref.dtype)


def _pool(u, prev16, zg, w_pool_all, scale_all, layer, pos0, tt=512):
    b, t, w = u.shape
    tt = min(tt, t)
    assert t % tt == 0 and tt % POOL_HALO == 0
    hb = tt // POOL_HALO
    kern = functools.partial(_pool_kernel, tt=tt, pos0=pos0)
    return pl.pallas_call(
        kern,
        grid=(b, t // tt),
        in_specs=[pl.BlockSpec((1, tt, w), lambda i, j: (i, j, 0)),
                  pl.BlockSpec((1, POOL_HALO, w), lambda i, j: (i, jnp.maximum(j * hb - 1, 0), 0)),
                  pl.BlockSpec((1, POOL_HALO, w), lambda i, j: (i, 0, 0)),
                  pl.BlockSpec((1, tt, w), lambda i, j: (i, j, 0)),
                  pl.BlockSpec((None, len(POOL_WINDOWS), POOL_GROUP_W, POOL_GROUP_W), lambda i, j: (layer, 0, 0, 0)),
                  pl.BlockSpec((None, 1, w), lambda i, j: (layer, 0, 0))],
        out_specs=pl.BlockSpec((1, tt, w), lambda i, j: (i, j, 0)),
        out_shape=jax.ShapeDtypeStruct((b, t, w), BF16),
        compiler_params=_cparams("parallel", "arbitrary"),
        name="pool",
    )(u, u, prev16, zg, w_pool_all, scale_all)


_NT = (((1,), (1,)), ((), ()))
_SIGN_BIT = -2 ** 31
LOG2_E = 1.4426950408889634


def _sb_kernel(q_ref, kd_ref, vd_ref, k_ref, v_ref, z_ref, trid_ref, tri_ref, o_ref, acc_ref, carry_ref, *,
               tq, tk, q_pos0, heads, cache_rows):
    i = pl.program_id(2)
    n_full = (q_pos0 + i * tq) // tk
    hs = range(heads)

    def lanes(h):
        return slice(h * SB_HEAD_DIM, (h + 1) * SB_HEAD_DIM)

    def block(k_of, v_of, tri, carry=None, valid=None):
        z2, suffix, total, out = {}, {}, {}, {}

        def score(h):
            z2[h] = lax.dot_general(q_ref[0, :, lanes(h)], k_of(h), _NT, preferred_element_type=F32)

        def suffix_sums(h):
            neg_abs = lax.bitcast_convert_type(lax.bitcast_convert_type(z2[h], jnp.int32) | _SIGN_BIT, F32)
            sp2 = jnp.maximum(z2[h], 0.0) + jnp.log(1.0 + jnp.exp2(neg_abs)) * LOG2_E
            if valid is not None:
                sp2 = jnp.where(valid, sp2, 0.0)
            suffix[h] = jnp.dot(sp2.astype(BF16), tri, preferred_element_type=F32)
            total[h] = jnp.sum(sp2, axis=1, keepdims=True)

        def weights(h):
            pre = z2[h] - suffix[h]
            if carry is not None:
                pre = pre - carry[h]
            if valid is not None:
                pre = jnp.where(valid, pre, -jnp.inf)
            out[h] = (jnp.dot(jnp.exp2(pre).astype(BF16), v_of(h), preferred_element_type=F32), total[h])

        for step in (score, suffix_sums, weights):
            for h in hs:
                step(h)
        return [out[h] for h in hs]

    def earlier(ref, kb):
        k0 = pl.multiple_of(kb * tk, tk)
        if cache_rows:
            base = pl.multiple_of(k0 * heads, tk * heads)
            return lambda h: ref[0, pl.ds(base + h, tk, stride=heads), :].astype(BF16)
        return lambda h: ref[0, pl.ds(k0, tk), lanes(h)]

    strictly_lower = (lax.broadcasted_iota(jnp.int32, (tq, tq), 1) < lax.broadcasted_iota(jnp.int32, (tq, tq), 0))
    own = block(lambda h: kd_ref[0, :, lanes(h)], lambda h: vd_ref[0, :, lanes(h)], trid_ref[...],
                valid=strictly_lower)
    for h, (pv, blocksum) in enumerate(own):
        acc_ref[h], carry_ref[h] = pv, blocksum

    @pl.loop(0, n_full)
    def _(t):
        kb = n_full - 1 - t
        carry = [carry_ref[h] for h in hs]
        for h, (pv, blocksum) in enumerate(
                block(earlier(k_ref, kb), earlier(v_ref, kb), tri_ref[...], carry=carry)):
            acc_ref[h] += pv
            carry_ref[h] = carry[h] + blocksum

    for h in range(heads):
        o_ref[0, :, lanes(h)] = (acc_ref[h] * _silu(z_ref[0, :, lanes(h)].astype(F32))).astype(o_ref.dtype)


def _suffix_sum_matrix(n):
    r = lax.broadcasted_iota(jnp.int32, (n, n), 0)
    c = lax.broadcasted_iota(jnp.int32, (n, n), 1)
    return (r >= c).astype(BF16)


def _stick_breaking(q, zg, k_new, v_new, past, tq, tk, heads):
    b, t, _ = q.shape
    tq = min(tq, t)
    w = heads * SB_HEAD_DIM
    assert t % tq == 0 and SB_HEADS % heads == 0
    if past is None:
        q_pos0, k_full, v_full = 0, k_new, v_new
        full_spec = pl.BlockSpec((1, t, w), lambda bi, h, i: (bi, 0, h))
    else:
        assert heads == SB_HEADS
        k_full, v_full, layer = past
        q_pos0 = k_full.shape[2] // SB_HEADS
        full_spec = pl.BlockSpec((None, 1, q_pos0 * SB_HEADS, SB_HEAD_DIM), lambda bi, h, i: (layer, bi, 0, 0))
    assert q_pos0 % tk == 0 and (t == tq or tq % tk == 0)
    tile = pl.BlockSpec((1, tq, w), lambda bi, h, i: (bi, i, h))
    kern = functools.partial(_sb_kernel, tq=tq, tk=tk, q_pos0=q_pos0, heads=heads, cache_rows=past is not None)
    return pl.pallas_call(
        kern,
        grid=(b, SB_HEADS // heads, t // tq),
        in_specs=[tile, tile, tile, full_spec, full_spec,
                  pl.BlockSpec((1, tq, w), lambda bi, h, i: (bi, i, BRANCH_W // w + h)),
                  pl.BlockSpec((tq, tq), lambda bi, h, i: (0, 0)),
                  pl.BlockSpec((tk, tk), lambda bi, h, i: (0, 0))],
        out_specs=tile,
        out_shape=jax.ShapeDtypeStruct((b, t, BRANCH_W), BF16),
        scratch_shapes=[pltpu.VMEM((heads, tq, SB_HEAD_DIM), F32), pltpu.VMEM((heads, tq, 1), F32)],
        compiler_params=_cparams("parallel", "parallel", "arbitrary"),
        name="stick_breaking",
    )(q, k_new, v_new, k_full, v_full, zg, _suffix_sum_matrix(tq), _suffix_sum_matrix(tk))


def _mlstm_kernel(q_ref, k_ref, v_ref, og_ref, z_ref, ifc_ref, ifr_ref, hn_ref, c0_ref, n0_ref, m0_ref,
                  y_ref, c_ref, n_ref, m_ref, *, chunk):
    ci = pl.program_id(1)

    @pl.when(ci == 0)
    def _():
        c_ref[...] = c0_ref[...]
        n_ref[...] = n0_ref[...]
        m_ref[...] = m0_ref[...]

    row = lax.broadcasted_iota(jnp.int32, (chunk, chunk), 0)
    col = lax.broadcasted_iota(jnp.int32, (chunk, chunk), 1)
    causal = col <= row
    log_k_scale = -0.5 * math.log(ML_HEAD_DIM)
    hs = range(q_ref.shape[0] * ML_HEADS)
    bi = [h // ML_HEADS for h in hs]
    hd = [h % ML_HEADS for h in hs]
    sls = [slice(hd[h] * ML_HEAD_DIM, (hd[h] + 1) * ML_HEAD_DIM) for h in hs]
    q = [q_ref[bi[h], :, sls[h]] for h in hs]
    k = [k_ref[bi[h], :, sls[h]] for h in hs]
    v = [v_ref[bi[h], :, sls[h]] for h in hs]
    c_prev = [c_ref[bi[h], hd[h]] for h in hs]
    n_prev = [n_ref[bi[h], hd[h]] for h in hs]
    m_prev = [m_ref[bi[h], hd[h], :, 0:1] for h in hs]
    qk = [lax.dot_general(q[h], k[h], _NT, preferred_element_type=F32) for h in hs]
    q_c = [jnp.dot(q[h], c_prev[h].astype(BF16), preferred_element_type=F32) for h in hs]

    f_col = [ifc_ref[bi[h], :, ML_HEADS + hd[h]:ML_HEADS + hd[h] + 1] for h in hs]
    f_row = [ifr_ref[bi[h], 0, ML_HEADS + hd[h]:ML_HEADS + hd[h] + 1, :] for h in hs]
    i_row = [ifr_ref[bi[h], 0, hd[h]:hd[h] + 1, :] for h in hs]
    bcum_col = [jnp.sum(jnp.where(causal, f_row[h], 0.0), axis=1, keepdims=True) for h in hs]
    bcum_row = [jnp.sum(jnp.where(row <= col, f_col[h], 0.0), axis=0, keepdims=True) for h in hs]
    d = [jnp.where(causal, bcum_col[h] - bcum_row[h] + i_row[h], -jnp.inf) for h in hs]
    g = [bcum_col[h] + m_prev[h] for h in hs]
    m_t = [jnp.maximum(g[h], jnp.max(d[h], axis=1, keepdims=True)) for h in hs]
    w_inter = [jnp.exp(g[h] - m_t[h]) for h in hs]
    s = [qk[h] * jnp.exp(d[h] - (m_t[h] - log_k_scale)) for h in hs]
    sv = [jnp.dot(s[h].astype(BF16), v[h], preferred_element_type=F32) for h in hs]

    i_col = [ifc_ref[bi[h], :, hd[h]:hd[h] + 1] for h in hs]
    b_last = [bcum_row[h][:, chunk - 1:chunk] for h in hs]
    m_new = [jnp.maximum(b_last[h] + m_prev[h],
                         jnp.max(b_last[h] - bcum_row[h] + i_row[h], axis=1, keepdims=True)) for h in hs]
    decay = [jnp.exp(b_last[h] + m_prev[h] - m_new[h]) for h in hs]
    w_end = [jnp.exp(b_last[h] - bcum_col[h] + i_col[h] - (m_new[h] - log_k_scale)) for h in hs]
    kw = [k[h].astype(F32) * w_end[h] for h in hs]
    kv = [lax.dot_general(kw[h].astype(BF16), v[h], (((0,), (0,)), ((), ())), preferred_element_type=F32)
          for h in hs]

    q_n = [jnp.sum(q[h].astype(F32) * n_prev[h], axis=1, keepdims=True) for h in hs]
    den = [jnp.sum(s[h], axis=1, keepdims=True) + w_inter[h] * q_n[h] for h in hs]
    h_out = [(sv[h] + w_inter[h] * q_c[h]) / jnp.maximum(jnp.abs(den[h]), jnp.exp(-m_t[h])) for h in hs]
    ms = [jnp.mean(h_out[h] * h_out[h], axis=1, keepdims=True) for h in hs]
    for h in hs:
        hn = h_out[h] * lax.rsqrt(ms[h] + RMS_EPS) * hn_ref[:, sls[h]]
        y = hn * _sigmoid(og_ref[bi[h], :, sls[h]].astype(F32))
        y_ref[bi[h], :, sls[h]] = (y * _silu(z_ref[bi[h], :, sls[h]].astype(F32))).astype(y_ref.dtype)

    for h in hs:
        c_ref[bi[h], hd[h]] = decay[h] * c_prev[h] + kv[h]
        n_ref[bi[h], hd[h]] = decay[h] * n_prev[h] + jnp.sum(kw[h], axis=0, keepdims=True)
        m_ref[bi[h], hd[h]] = jnp.broadcast_to(m_new[h], (1, LANES))


def _mlstm(ml, zg, ifc, hnorm_all, layer, c0, n0, m0, chunk, nb=1):
    b, t, _ = ml.shape
    chunk = min(chunk, t)
    assert t % chunk == 0 and b % nb == 0
    nc = t // chunk
    ifr = jnp.transpose(ifc[:, :, :2 * ML_HEADS].reshape(b, nc, chunk, 2 * ML_HEADS), (0, 1, 3, 2))
    kern = functools.partial(_mlstm_kernel, chunk=chunk)

    def colblock(cb):
        return pl.BlockSpec((nb, chunk, BRANCH_W), lambda bi, ci: (bi, ci, cb))

    def state(shape):
        return pl.BlockSpec((nb,) + shape, lambda bi, ci: (bi,) + (0,) * len(shape))

    c_shape = (ML_HEADS, ML_HEAD_DIM, ML_HEAD_DIM)
    n_shape = (ML_HEADS, 1, ML_HEAD_DIM)
    m_shape = (ML_HEADS, 1, LANES)
    return pl.pallas_call(
        kern,
        grid=(b // nb, nc),
        in_specs=[colblock(0), colblock(1), colblock(2), colblock(3), colblock(2),
                  pl.BlockSpec((nb, chunk, LANES), lambda bi, ci: (bi, ci, 0)),
                  pl.BlockSpec((nb, 1, 2 * ML_HEADS, chunk), lambda bi, ci: (bi, ci, 0, 0)),
                  pl.BlockSpec((None, 1, BRANCH_W), lambda bi, ci: (layer, 0, 0)),
                  state(c_shape), state(n_shape), state(m_shape)],
        out_specs=[pl.BlockSpec((nb, chunk, BRANCH_W), lambda bi, ci: (bi, ci, 0)),
                   state(c_shape), state(n_shape), state(m_shape)],
        out_shape=[jax.ShapeDtypeStruct((b, t, BRANCH_W), BF16),
                   jax.ShapeDtypeStruct((b,) + c_shape, F32),
                   jax.ShapeDtypeStruct((b,) + n_shape, F32),
                   jax.ShapeDtypeStruct((b,) + m_shape, F32)],
        compiler_params=_cparams("parallel", "arbitrary"),
        name="mlstm",
    )(ml, ml, ml, ml, zg, ifc, ifr, hnorm_all, c0, n0, m0)


def _merge_kernel(y0_ref, y1_ref, y2_ref, g0_ref, g1_ref, g2_ref, w_ref, o_ref):
    acc = None
    for bi, (y_ref, g_ref) in enumerate(((y0_ref, g0_ref), (y1_ref, g1_ref), (y2_ref, g2_ref))):
        term = _sigmoid(g_ref[...].astype(F32)) * jnp.dot(y_ref[...], w_ref[bi], preferred_element_type=F32)
        acc = term if acc is None else acc + term
    o_ref[...] = acc.astype(o_ref.dtype)


def _merge(ys, zg2d, w_branch_all, layer, tm=1024, tn=1024):
    m = zg2d.shape[0]
    tm = min(tm, m)
    assert m % tm == 0 and D_MODEL % tn == 0 and ZG_GATE0 % tn == 0
    gb = ZG_GATE0 // tn

    def gate(bi):
        return pl.BlockSpec((tm, tn), lambda i, j: (i, gb + bi * (D_MODEL // tn) + j))

    y_spec = pl.BlockSpec((tm, BRANCH_W), lambda i, j: (i, 0))
    return pl.pallas_call(
        _merge_kernel,
        grid=(m // tm, D_MODEL // tn),
        in_specs=[y_spec, y_spec, y_spec, gate(0), gate(1), gate(2),
                  pl.BlockSpec((None, N_BRANCH, BRANCH_W, tn), lambda i, j: (layer, 0, 0, j))],
        out_specs=pl.BlockSpec((tm, tn), lambda i, j: (i, j)),
        out_shape=jax.ShapeDtypeStruct((m, D_MODEL), BF16),
        compiler_params=_cparams("parallel", "arbitrary"),
        name="merge",
    )(*ys, zg2d, zg2d, zg2d, w_branch_all)


def _out_kernel(a_ref, w_ref, x_ref, gpost_ref, gnext_ref, xo_ref, xn_ref):
    y = jnp.dot(a_ref[...], w_ref[...], preferred_element_type=F32)
    yn = y * lax.rsqrt(jnp.mean(y * y, axis=-1, keepdims=True) + RMS_EPS) * gpost_ref[...]
    xo = x_ref[...] + yn
    xo_ref[...] = xo
    xn = xo * lax.rsqrt(jnp.mean(xo * xo, axis=-1, keepdims=True) + RMS_EPS) * gnext_ref[...]
    xn_ref[...] = xn.astype(xn_ref.dtype)


def _out_proj(merged, w_out_all, x, g_post_all, g_pre_all, layer, tm=512):
    m = x.shape[0]
    tm = min(tm, m)
    assert m % tm == 0
    nxt = (layer + 1) % g_pre_all.shape[0]
    row = pl.BlockSpec((tm, D_MODEL), lambda i: (i, 0))
    return pl.pallas_call(
        _out_kernel,
        grid=(m // tm,),
        in_specs=[row,
                  pl.BlockSpec((None, D_MODEL, D_MODEL), lambda i: (layer, 0, 0)),
                  row,
                  pl.BlockSpec((None, 1, D_MODEL), lambda i: (layer, 0, 0)),
                  pl.BlockSpec((None, 1, D_MODEL), lambda i: (nxt, 0, 0))],
        out_specs=[row, row],
        out_shape=[jax.ShapeDtypeStruct((m, D_MODEL), F32), jax.ShapeDtypeStruct((m, D_MODEL), BF16)],
        compiler_params=_cparams("parallel"),
        name="out_proj",
    )(merged, w_out_all, x, g_post_all, g_pre_all)


def _trunk_layer(x, xn, shape, pos0, pool_prev16, kv_past, kv_new, c0, n0, m0, params, layer,
                 sb_tk=256, sb_tq=256, sb_heads=8, ml_chunk=256):
    b, t = shape
    wa, wz, w_if, b_if_pad, w_pool, pool_scale, hnorm, w_branch, w_out, g_post, g_pre = params
    u, ifc = _pool_if_proj(xn, wa, w_if, b_if_pad, layer)
    (q16,) = _mm(xn, wa, layer, OFF_SB_Q, BRANCH_W, (BF16,))
    k_all, k16 = _mm_heads(xn, wa, layer, OFF_SB_K, kv_new[0])
    v_all, v16 = _mm_heads(xn, wa, layer, OFF_SB_V, kv_new[1])
    (ml,) = _mm(xn, wa, layer, OFF_ML_Q, ML_COLS, (BF16,))
    (zg,) = _mm(xn, wz, layer, 0, ZG_COLS, (BF16,))

    u3 = u.reshape(b, t, BRANCH_W)
    zg3 = zg.reshape(b, t, ZG_COLS)
    y_pool = _pool(u3, pool_prev16, zg3, w_pool, pool_scale, layer, pos0)

    past = None if kv_past is None else kv_past + (layer,)
    y_sb = _stick_breaking(q16.reshape(b, t, BRANCH_W), zg3, k16.reshape(b, t, BRANCH_W),
                           v16.reshape(b, t, BRANCH_W), past, sb_tq, sb_tk, sb_heads)

    y_ml, c_new, n_new, m_new = _mlstm(ml.reshape(b, t, ML_COLS), zg3, ifc.reshape(b, t, LANES), hnorm, layer,
                                       c0, n0, m0, ml_chunk)

    ys = [y.reshape(b * t, BRANCH_W) for y in (y_pool, y_sb, y_ml)]
    merged = _merge(ys, zg, w_branch, layer)
    x_out, xn_next = _out_proj(merged, w_out, x, g_post, g_pre, layer)

    pool_new = u3[:, t - POOL_STATE:, :]
    return x_out, xn_next, (k_all, v_all), (pool_new, c_new, n_new[:, :, 0, :], m_new[:, :, 0, 0])


def _cast_kernel(x_ref, o_ref):
    o_ref[...] = x_ref[...].astype(o_ref.dtype)


def _scaled_cast_kernel(x_ref, s_ref, o_ref):
    o_ref[...] = (x_ref[...] * s_ref[...]).astype(o_ref.dtype)


def _cast_bf16(x, col0, ncols, col_scale=None, tr=512, tc=2048):
    d, r, _ = x.shape
    tr = min(tr, r)
    assert r % tr == 0 and ncols % tc == 0 and col0 % tc == 0
    cb = col0 // tc
    blk = pl.BlockSpec((None, tr, tc), lambda i, j, k: (i, j, k))
    in_specs, args, kern = [pl.BlockSpec((None, tr, tc), lambda i, j, k: (i, j, cb + k))], [x], _cast_kernel
    if col_scale is not None:
        in_specs.append(pl.BlockSpec((1, tc), lambda i, j, k: (0, k)))
        args.append(col_scale)
        kern = _scaled_cast_kernel
    return pl.pallas_call(
        kern,
        grid=(d, r // tr, ncols // tc),
        in_specs=in_specs,
        out_specs=blk,
        out_shape=jax.ShapeDtypeStruct((d, r, ncols), BF16),
        compiler_params=_cparams("parallel", "parallel", "parallel"),
        name="cast_bf16",
    )(*args)


def _shifted_cast_kernel(x_ref, nxt_ref, o_ref, *, shift):
    x = jnp.concatenate([x_ref[:, shift:], nxt_ref[:, :shift]], axis=1)
    o_ref[...] = x.astype(o_ref.dtype)


def _shifted_cast_bf16(x, col0, shift, ncols, tr=512, tc=1024):
    d, r, c = x.shape
    assert r % tr == 0 and ncols % tc == 0 and col0 % tc == 0 and 0 < shift < LANES
    assert col0 + shift + ncols <= c
    cb, nb = col0 // tc, tc // LANES
    return pl.pallas_call(
        functools.partial(_shifted_cast_kernel, shift=shift),
        grid=(d, r // tr, ncols // tc),
        in_specs=[pl.BlockSpec((None, tr, tc), lambda i, j, k: (i, j, cb + k)),
                  pl.BlockSpec((None, tr, LANES), lambda i, j, k: (i, j, (cb + k + 1) * nb))],
        out_specs=pl.BlockSpec((None, tr, tc), lambda i, j, k: (i, j, k)),
        out_shape=jax.ShapeDtypeStruct((d, r, ncols), BF16),
        compiler_params=_cparams("parallel", "parallel", "parallel"),
        name="shifted_cast_bf16",
    )(x, x)


def _prep_params(g_pre, w_in, b_if, w_pool, pool_scale, ml_hnorm, w_branch, w_out, g_post):
    depth = w_in.shape[0]
    q_scale = SB_HEAD_DIM ** -0.5 * LOG2_E
    cols = lax.broadcasted_iota(jnp.int32, (1, OFF_ML_I), 1)
    col_scale = jnp.where((cols >= OFF_SB_Q) & (cols < OFF_SB_K), q_scale, 1.0).astype(F32)
    wa = _cast_bf16(w_in, 0, OFF_ML_I, col_scale)
    wz = _shifted_cast_bf16(w_in, OFF_ML_I, OFF_Z - OFF_ML_I, ZG_COLS)
    gate_cols = (lax.broadcasted_iota(jnp.int32, (1, LANES), 1) < 2 * ML_HEADS).astype(F32)
    w_if = _cast_bf16(w_in, OFF_ML_I, LANES, gate_cols, tc=LANES)
    b_if_pad = jnp.pad(b_if.reshape(depth, 1, 2 * ML_HEADS), ((0, 0), (0, 0), (0, LANES - 2 * ML_HEADS)))
    w_branch16 = _cast_bf16(w_branch.reshape(depth * N_BRANCH, BRANCH_W, D_MODEL), 0, D_MODEL)
    w_out16 = _cast_bf16(w_out, 0, D_MODEL)
    return (wa, wz, w_if, b_if_pad, w_pool.astype(BF16), pool_scale.reshape(depth, 1, BRANCH_W),
            ml_hnorm.reshape(depth, 1, BRANCH_W), w_branch16.reshape(depth, N_BRANCH, BRANCH_W, D_MODEL), w_out16,
            g_post.reshape(depth, 1, D_MODEL), g_pre.reshape(depth, 1, D_MODEL))


def kernel(x_prompt, x_sample, cache_sb_k, cache_sb_v, state_pool, state_ml_c, state_ml_n, state_ml_m,
           g_pre, w_in, b_if, w_pool, pool_scale, ml_hnorm, w_branch, w_out, g_post):
    bp, tp, _ = x_prompt.shape
    bs, ts, _ = x_sample.shape
    past_len = cache_sb_k.shape[2]
    params = _prep_params(g_pre, w_in, b_if, w_pool, pool_scale, ml_hnorm, w_branch, w_out, g_post)
    g_pre3 = params[-1]

    xp = x_prompt.reshape(bp * tp, D_MODEL)
    xs = x_sample.reshape(bs * ts, D_MODEL)
    xnp = _rmsnorm(xp, g_pre3, 0)
    xns = _rmsnorm(xs, g_pre3, 0)

    zero_pool = jnp.zeros((bp, POOL_HALO, BRANCH_W), F32)
    zero_c = jnp.zeros((bp, ML_HEADS, ML_HEAD_DIM, ML_HEAD_DIM), F32)
    zero_n = jnp.zeros((bp, ML_HEADS, 1, ML_HEAD_DIM), F32)
    zero_m = jnp.zeros((bp, ML_HEADS, 1, LANES), F32)
    pool_prev = jnp.pad(state_pool, ((0, 0), (0, 0), (POOL_HALO - POOL_STATE, 0), (0, 0)))
    n_prev = state_ml_n[:, :, :, None, :]
    m_prev = jnp.broadcast_to(state_ml_m[:, :, :, None, None], state_ml_m.shape + (1, LANES))

    kv_past = (cache_sb_k.reshape(DEPTH, bs, past_len * SB_HEADS, SB_HEAD_DIM),
               cache_sb_v.reshape(DEPTH, bs, past_len * SB_HEADS, SB_HEAD_DIM))

    kv_p = tuple(jnp.zeros((DEPTH, bp * tp * SB_HEADS, SB_HEAD_DIM), F32) for _ in range(2))
    kv_s = tuple(jnp.zeros((DEPTH, bs * ts * SB_HEADS, SB_HEAD_DIM), F32) for _ in range(2))

    pr, sa = [], []
    for l in range(DEPTH):
        xp, xnp, kv_p, outs_p = _trunk_layer(xp, xnp, (bp, tp), 0, zero_pool, None, kv_p,
                                             zero_c, zero_n, zero_m, params, l)
        xs, xns, kv_s, outs_s = _trunk_layer(xs, xns, (bs, ts), past_len, pool_prev[l], kv_past, kv_s,
                                             state_ml_c[l], n_prev[l], m_prev[l], params, l, sb_tq=ts)
        pr.append(outs_p)
        sa.append(outs_s)
    p_pool, p_c, p_n, p_m = [jnp.stack(a, axis=0) for a in zip(*pr)]
    s_pool, s_c, s_n, s_m = [jnp.stack(a, axis=0) for a in zip(*sa)]
    p_k, p_v = [a.reshape(DEPTH, bp, tp, SB_HEADS, SB_HEAD_DIM) for a in kv_p]
    s_k, s_v = [a.reshape(DEPTH, bs, ts, SB_HEADS, SB_HEAD_DIM) for a in kv_s]
    return (xp.reshape(bp, tp, D_MODEL), xs.reshape(bs, ts, D_MODEL),
            p_k, p_v, p_pool, p_c, p_n, p_m, s_k, s_v, s_pool, s_c, s_n, s_m)
```

```python
import functools
import math

import jax
import jax.numpy as jnp
from jax import lax
from jax.experimental import pallas as pl
from jax.experimental.pallas import tpu as pltpu

F32 = jnp.float32
BF16 = jnp.bfloat16

D_MODEL = 2048
DEPTH = 4
BRANCH_W = 1024
N_BRANCH = 3
POOL_WINDOWS = (2, 4, 8, 16)
POOL_GROUP_W = BRANCH_W // len(POOL_WINDOWS)
POOL_STATE = 15
POOL_HALO = 16
SB_HEADS = 8
SB_HEAD_DIM = BRANCH_W // SB_HEADS
ML_HEADS = 4
ML_HEAD_DIM = BRANCH_W // ML_HEADS
RMS_EPS = 1e-6
LANES = 128

OFF_POOL, OFF_SB_Q, OFF_SB_K, OFF_SB_V = 0, 1024, 2048, 3072
OFF_ML_Q, OFF_ML_K, OFF_ML_V, OFF_ML_O = 4096, 5120, 6144, 7168
OFF_ML_I = 8192
OFF_Z = OFF_ML_I + 2 * ML_HEADS
OFF_GATE = OFF_Z + N_BRANCH * BRANCH_W
IN_COLS = OFF_GATE + N_BRANCH * D_MODEL

ML_COLS = OFF_ML_I - OFF_ML_Q
ZG_COLS = IN_COLS - OFF_Z
ZG_GATE0 = N_BRANCH * BRANCH_W

VMEM_LIMIT = 56 * 1024 * 1024


def _cparams(*sem):
    return pltpu.CompilerParams(dimension_semantics=sem, vmem_limit_bytes=VMEM_LIMIT)


def _sigmoid(x):
    return 1.0 / (1.0 + jnp.exp(-x))


def _silu(x):
    return x * _sigmoid(x)


def _rmsnorm_kernel(x_ref, g_ref, o_ref):
    x = x_ref[...]
    ms = jnp.mean(x * x, axis=-1, keepdims=True)
    o_ref[...] = (x * lax.rsqrt(ms + RMS_EPS) * g_ref[...]).astype(o_ref.dtype)


def _rmsnorm(x, g_all, layer, tm=512):
    n, d = x.shape
    tm = min(tm, n)
    return pl.pallas_call(
        _rmsnorm_kernel,
        grid=(n // tm,),
        in_specs=[pl.BlockSpec((tm, d), lambda i: (i, 0)),
                  pl.BlockSpec((None, 1, d), lambda i: (layer, 0, 0))],
        out_specs=pl.BlockSpec((tm, d), lambda i: (i, 0)),
        out_shape=jax.ShapeDtypeStruct((n, d), BF16),
        compiler_params=_cparams("parallel"),
        name="rmsnorm",
    )(x, g_all)


def _mm_kernel(a_ref, b_ref, *o_refs):
    acc = jnp.dot(a_ref[...], b_ref[...], preferred_element_type=F32)
    for o_ref in o_refs:
        o_ref[...] = acc.astype(o_ref.dtype)


def _mm(a, w_all, layer, col0, ncols, out_dtypes, tm=1024, tn=1024):
    m, k = a.shape
    tm, tn = min(tm, m), min(tn, ncols)
    assert m % tm == 0 and ncols % tn == 0 and col0 % tn == 0
    cb = col0 // tn
    outs = [jax.ShapeDtypeStruct((m, ncols), dt) for dt in out_dtypes]
    res = pl.pallas_call(
        _mm_kernel,
        grid=(m // tm, ncols // tn),
        in_specs=[pl.BlockSpec((tm, k), lambda i, j: (i, 0)),
                  pl.BlockSpec((None, k, tn), lambda i, j: (layer, 0, j + cb))],
        out_specs=[pl.BlockSpec((tm, tn), lambda i, j: (i, j)) for _ in outs],
        out_shape=outs,
        compiler_params=_cparams("parallel", "arbitrary"),
        name="in_proj",
    )(a, w_all)
    return res


def _mm_heads_kernel(a_ref, b_ref, stacked_ref, o32_ref, o16_ref):
    del stacked_ref
    acc = jnp.dot(a_ref[...], b_ref[...], preferred_element_type=F32)
    tm = acc.shape[0]
    for h in range(SB_HEADS):
        o32_ref[pl.ds(h, tm, stride=SB_HEADS), :] = acc[:, h * SB_HEAD_DIM:(h + 1) * SB_HEAD_DIM]
    o16_ref[...] = acc.astype(o16_ref.dtype)


def _mm_heads(a, w_all, layer, col0, stacked, tm=1024):
    m, k = a.shape
    tm = min(tm, m)
    assert m % tm == 0 and col0 % BRANCH_W == 0 and stacked.shape[1:] == (m * SB_HEADS, SB_HEAD_DIM)
    cb = col0 // BRANCH_W
    return pl.pallas_call(
        _mm_heads_kernel,
        grid=(m // tm,),
        in_specs=[pl.BlockSpec((tm, k), lambda i: (i, 0)),
                  pl.BlockSpec((None, k, BRANCH_W), lambda i: (layer, 0, cb)),
                  pl.BlockSpec(memory_space=pl.ANY)],
        out_specs=[pl.BlockSpec((None, tm * SB_HEADS, SB_HEAD_DIM), lambda i: (layer, i, 0)),
                   pl.BlockSpec((tm, BRANCH_W), lambda i: (i, 0))],
        out_shape=[jax.ShapeDtypeStruct(stacked.shape, F32),
                   jax.ShapeDtypeStruct((m, BRANCH_W), BF16)],
        input_output_aliases={2: 0},
        compiler_params=_cparams("parallel"),
        name="in_proj_heads",
    )(a, w_all, stacked)


def _pool_if_kernel(a_ref, wu_ref, wq_ref, wif_ref, b_ref, u_ref, q_ref, if_ref):
    a = a_ref[...]
    u_ref[...] = jnp.dot(a, wu_ref[...], preferred_element_type=F32)
    q_ref[...] = jnp.dot(a, wq_ref[...], preferred_element_type=F32).astype(q_ref.dtype)
    acc = jnp.dot(a, wif_ref[...], preferred_element_type=F32) + b_ref[...]
    col = lax.broadcasted_iota(jnp.int32, acc.shape, 1)
    log_sig = jnp.minimum(acc, 0.0) - jnp.log(1.0 + jnp.exp(-jnp.abs(acc)))
    if_ref[...] = jnp.where(col >= ML_HEADS, log_sig, acc)


def _pool_if_proj(a, wa, w_if, b_if_pad, layer, tm=1024):
    m, k = a.shape
    tm = min(tm, m)
    assert m % tm == 0
    wide = pl.BlockSpec((tm, BRANCH_W), lambda i: (i, 0))
    return pl.pallas_call(
        _pool_if_kernel,
        grid=(m // tm,),
        in_specs=[pl.BlockSpec((tm, k), lambda i: (i, 0)),
                  pl.BlockSpec((None, k, BRANCH_W), lambda i: (layer, 0, OFF_POOL // BRANCH_W)),
                  pl.BlockSpec((None, k, BRANCH_W), lambda i: (layer, 0, OFF_SB_Q // BRANCH_W)),
                  pl.BlockSpec((None, k, LANES), lambda i: (layer, 0, 0)),
                  pl.BlockSpec((None, 1, LANES), lambda i: (layer, 0, 0))],
        out_specs=[wide, wide, pl.BlockSpec((tm, LANES), lambda i: (i, 0))],
        out_shape=[jax.ShapeDtypeStruct((m, BRANCH_W), F32), jax.ShapeDtypeStruct((m, BRANCH_W), BF16),
                   jax.ShapeDtypeStruct((m, LANES), F32)],
        compiler_params=_cparams("parallel"),
        name="pool_if_proj",
    )(a, wa, wa, w_if, b_if_pad)


def _pool_kernel(u_ref, halo_ref, prev_ref, z_ref, w_ref, s_ref, o_ref, *, tt, pos0):
    t = pl.program_id(1)
    halo = jnp.where(t == 0, prev_ref[0], halo_ref[0])
    u = u_ref[0]
    ys = []
    for gi, w in enumerate(POOL_WINDOWS):
        lo = gi * POOL_GROUP_W
        x = jnp.concatenate([halo[:, lo:lo + POOL_GROUP_W], u[:, lo:lo + POOL_GROUP_W]], axis=0)
        s = 1
        while s < w:
            x = x + pltpu.roll(x, s, axis=0)
            s *= 2
        wsum = x[POOL_HALO:, :]
        if pos0 + 1 >= w:
            mean = wsum * (1.0 / w)
        else:
            pos = pos0 + t * tt + lax.broadcasted_iota(jnp.int32, (tt, 1), 0)
            mean = wsum / jnp.minimum(pos + 1, w).astype(F32)
        pooled = mean - u[:, lo:lo + POOL_GROUP_W]
        ys.append(jnp.dot(pooled.astype(BF16), w_ref[gi], preferred_element_type=F32))
    y = jnp.concatenate(ys, axis=-1) * s_ref[...]
    o_ref[0] = (y * _silu(z_ref[0].astype(F32))).astype(o_ref.dtype)


def _pool(u, prev16, zg, w_pool_all, scale_all, layer, pos0, tt=512):
    b, t, w = u.shape
    tt = min(tt, t)
    assert t % tt == 0 and tt % POOL_HALO == 0
    hb = tt // POOL_HALO
    kern = functools.partial(_pool_kernel, tt=tt, pos0=pos0)
    return pl.pallas_call(
        kern,
        grid=(b, t // tt),
        in_specs=[pl.BlockSpec((1, tt, w), lambda i, j: (i, j, 0)),
                  pl.BlockSpec((1, POOL_HALO, w), lambda i, j: (i, jnp.maximum(j * hb - 1, 0), 0)),
                  pl.BlockSpec((1, POOL_HALO, w), lambda i, j: (i, 0, 0)),
                  pl.BlockSpec((1, tt, w), lambda i, j: (i, j, 0)),
                  pl.BlockSpec((None, len(POOL_WINDOWS), POOL_GROUP_W, POOL_GROUP_W), lambda i, j: (layer, 0, 0, 0)),
                  pl.BlockSpec((None, 1, w), lambda i, j: (layer, 0, 0))],
        out_specs=pl.BlockSpec((1, tt, w), lambda i, j: (i, j, 0)),
        out_shape=jax.ShapeDtypeStruct((b, t, w), BF16),
        compiler_params=_cparams("parallel", "arbitrary"),
        name="pool",
    )(u, u, prev16, zg, w_pool_all, scale_all)


_NT = (((1,), (1,)), ((), ()))
_SIGN_BIT = -2 ** 31
LOG2_E = 1.4426950408889634


def _sb_kernel(q_ref, kd_ref, vd_ref, k_ref, v_ref, z_ref, trid_ref, tri_ref, o_ref, acc_ref, carry_ref, *,
               tq, tk, q_pos0, heads, cache_rows):
    i = pl.program_id(2)
    n_full = (q_pos0 + i * tq) // tk
    hs = range(heads)

    def lanes(h):
        return slice(h * SB_HEAD_DIM, (h + 1) * SB_HEAD_DIM)

    def block(k_of, v_of, tri, carry=None, valid=None):
        z2, suffix, total, out = {}, {}, {}, {}

        def score(h):
            z2[h] = lax.dot_general(q_ref[0, :, lanes(h)], k_of(h), _NT, preferred_element_type=F32)

        def suffix_sums(h):
            neg_abs = lax.bitcast_convert_type(lax.bitcast_convert_type(z2[h], jnp.int32) | _SIGN_BIT, F32)
            sp2 = jnp.maximum(z2[h], 0.0) + jnp.log(1.0 + jnp.exp2(neg_abs)) * LOG2_E
            if valid is not None:
                sp2 = jnp.where(valid, sp2, 0.0)
            suffix[h] = jnp.dot(sp2.astype(BF16), tri, preferred_element_type=F32)
            total[h] = jnp.sum(sp2, axis=1, keepdims=True)

        def weights(h):
            pre = z2[h] - suffix[h]
            if carry is not None:
                pre = pre - carry[h]
            if valid is not None:
                pre = jnp.where(valid, pre, -jnp.inf)
            out[h] = (jnp.dot(jnp.exp2(pre).astype(BF16), v_of(h), preferred_element_type=F32), total[h])

        for step in (score, suffix_sums, weights):
            for h in hs:
                step(h)
        return [out[h] for h in hs]

    def earlier(ref, kb):
        k0 = pl.multiple_of(kb * tk, tk)
        if cache_rows:
            base = pl.multiple_of(k0 * heads, tk * heads)
            return lambda h: ref[0, pl.ds(base + h, tk, stride=heads), :].astype(BF16)
        return lambda h: ref[0, pl.ds(k0, tk), lanes(h)]

    strictly_lower = (lax.broadcasted_iota(jnp.int32, (tq, tq), 1) < lax.broadcasted_iota(jnp.int32, (tq, tq), 0))
    own = block(lambda h: kd_ref[0, :, lanes(h)], lambda h: vd_ref[0, :, lanes(h)], trid_ref[...],
                valid=strictly_lower)
    for h, (pv, blocksum) in enumerate(own):
        acc_ref[h], carry_ref[h] = pv, blocksum

    @pl.loop(0, n_full)
    def _(t):
        kb = n_full - 1 - t
        carry = [carry_ref[h] for h in hs]
        for h, (pv, blocksum) in enumerate(
                block(earlier(k_ref, kb), earlier(v_ref, kb), tri_ref[...], carry=carry)):
            acc_ref[h] += pv
            carry_ref[h] = carry[h] + blocksum

    for h in range(heads):
        o_ref[0, :, lanes(h)] = (acc_ref[h] * _silu(z_ref[0, :, lanes(h)].astype(F32))).astype(o_ref.dtype)


def _suffix_sum_matrix(n):
    r = lax.broadcasted_iota(jnp.int32, (n, n), 0)
    c = lax.broadcasted_iota(jnp.int32, (n, n), 1)
    return (r >= c).astype(BF16)


def _stick_breaking(q, zg, k_new, v_new, past, tq, tk, heads):
    b, t, _ = q.shape
    tq = min(tq, t)
    w = heads * SB_HEAD_DIM
    assert t % tq == 0 and SB_HEADS % heads == 0
    if past is None:
        q_pos0, k_full, v_full = 0, k_new, v_new
        full_spec = pl.BlockSpec((1, t, w), lambda bi, h, i: (bi, 0, h))
    else:
        assert heads == SB_HEADS
        k_full, v_full, layer = past
        q_pos0 = k_full.shape[2] // SB_HEADS
        full_spec = pl.BlockSpec((None, 1, q_pos0 * SB_HEADS, SB_HEAD_DIM), lambda bi, h, i: (layer, bi, 0, 0))
    assert q_pos0 % tk == 0 and (t == tq or tq % tk == 0)
    tile = pl.BlockSpec((1, tq, w), lambda bi, h, i: (bi, i, h))
    kern = functools.partial(_sb_kernel, tq=tq, tk=tk, q_pos0=q_pos0, heads=heads, cache_rows=past is not None)
    return pl.pallas_call(
        kern,
        grid=(b, SB_HEADS // heads, t // tq),
        in_specs=[tile, tile, tile, full_spec, full_spec,
                  pl.BlockSpec((1, tq, w), lambda bi, h, i: (bi, i, BRANCH_W // w + h)),
                  pl.BlockSpec((tq, tq), lambda bi, h, i: (0, 0)),
                  pl.BlockSpec((tk, tk), lambda bi, h, i: (0, 0))],
        out_specs=tile,
        out_shape=jax.ShapeDtypeStruct((b, t, BRANCH_W), BF16),
        scratch_shapes=[pltpu.VMEM((heads, tq, SB_HEAD_DIM), F32), pltpu.VMEM((heads, tq, 1), F32)],
        compiler_params=_cparams("parallel", "parallel", "arbitrary"),
        name="stick_breaking",
    )(q, k_new, v_new, k_full, v_full, zg, _suffix_sum_matrix(tq), _suffix_sum_matrix(tk))


def _mlstm_kernel(q_ref, k_ref, v_ref, og_ref, z_ref, ifc_ref, ifr_ref, hn_ref, c0_ref, n0_ref, m0_ref,
                  y_ref, c_ref, n_ref, m_ref, *, chunk):
    ci = pl.program_id(1)

    @pl.when(ci == 0)
    def _():
        c_ref[...] = c0_ref[...]
        n_ref[...] = n0_ref[...]
        m_ref[...] = m0_ref[...]

    row = lax.broadcasted_iota(jnp.int32, (chunk, chunk), 0)
    col = lax.broadcasted_iota(jnp.int32, (chunk, chunk), 1)
    causal = col <= row
    log_k_scale = -0.5 * math.log(ML_HEAD_DIM)
    hs = range(q_ref.shape[0] * ML_HEADS)
    bi = [h // ML_HEADS for h in hs]
    hd = [h % ML_HEADS for h in hs]
    sls = [slice(hd[h] * ML_HEAD_DIM, (hd[h] + 1) * ML_HEAD_DIM) for h in hs]
    q = [q_ref[bi[h], :, sls[h]] for h in hs]
    k = [k_ref[bi[h], :, sls[h]] for h in hs]
    v = [v_ref[bi[h], :, sls[h]] for h in hs]
    c_prev = [c_ref[bi[h], hd[h]] for h in hs]
    n_prev = [n_ref[bi[h], hd[h]] for h in hs]
    m_prev = [m_ref[bi[h], hd[h], :, 0:1] for h in hs]
    qk = [lax.dot_general(q[h], k[h], _NT, preferred_element_type=F32) for h in hs]
    q_c = [jnp.dot(q[h], c_prev[h].astype(BF16), preferred_element_type=F32) for h in hs]

    f_col = [ifc_ref[bi[h], :, ML_HEADS + hd[h]:ML_HEADS + hd[h] + 1] for h in hs]
    f_row = [ifr_ref[bi[h], 0, ML_HEADS + hd[h]:ML_HEADS + hd[h] + 1, :] for h in hs]
    i_row = [ifr_ref[bi[h], 0, hd[h]:hd[h] + 1, :] for h in hs]
    bcum_col = [jnp.sum(jnp.where(causal, f_row[h], 0.0), axis=1, keepdims=True) for h in hs]
    bcum_row = [jnp.sum(jnp.where(row <= col, f_col[h], 0.0), axis=0, keepdims=True) for h in hs]
    d = [jnp.where(causal, bcum_col[h] - bcum_row[h] + i_row[h], -jnp.inf) for h in hs]
    g = [bcum_col[h] + m_prev[h] for h in hs]
    m_t = [jnp.maximum(g[h], jnp.max(d[h], axis=1, keepdims=True)) for h in hs]
    w_inter = [jnp.exp(g[h] - m_t[h]) for h in hs]
    s = [qk[h] * jnp.exp(d[h] - (m_t[h] - log_k_scale)) for h in hs]
    sv = [jnp.dot(s[h].astype(BF16), v[h], preferred_element_type=F32) for h in hs]

    i_col = [ifc_ref[bi[h], :, hd[h]:hd[h] + 1] for h in hs]
    b_last = [bcum_row[h][:, chunk - 1:chunk] for h in hs]
    m_new = [jnp.maximum(b_last[h] + m_prev[h],
                         jnp.max(b_last[h] - bcum_row[h] + i_row[h], axis=1, keepdims=True)) for h in hs]
    decay = [jnp.exp(b_last[h] + m_prev[h] - m_new[h]) for h in hs]
    w_end = [jnp.exp(b_last[h] - bcum_col[h] + i_col[h] - (m_new[h] - log_k_scale)) for h in hs]
    kw = [k[h].astype(F32) * w_end[h] for h in hs]
    kv = [lax.dot_general(kw[h].astype(BF16), v[h], (((0,), (0,)), ((), ())), preferred_element_type=F32)
          for h in hs]

    q_n = [jnp.sum(q[h].astype(F32) * n_prev[h], axis=1, keepdims=True) for h in hs]
    den = [jnp.sum(s[h], axis=1, keepdims=True) + w_inter[h] * q_n[h] for h in hs]
    h_out = [(sv[h] + w_inter[h] * q_c[h]) / jnp.maximum(jnp.abs(den[h]), jnp.exp(-m_t[h])) for h in hs]
    ms = [jnp.mean(h_out[h] * h_out[h], axis=1, keepdims=True) for h in hs]
    for h in hs:
        hn = h_out[h] * lax.rsqrt(ms[h] + RMS_EPS) * hn_ref[:, sls[h]]
        y = hn * _sigmoid(og_ref[bi[h], :, sls[h]].astype(F32))
        y_ref[bi[h], :, sls[h]] = (y * _silu(z_ref[bi[h], :, sls[h]].astype(F32))).astype(y_ref.dtype)

    for h in hs:
        c_ref[bi[h], hd[h]] = decay[h] * c_prev[h] + kv[h]
        n_ref[bi[h], hd[h]] = decay[h] * n_prev[h] + jnp.sum(kw[h], axis=0, keepdims=True)
        m_ref[bi[h], hd[h]] = jnp.broadcast_to(m_new[h], (1, LANES))


def _mlstm(ml, zg, ifc, hnorm_all, layer, c0, n0, m0, chunk, nb=1):
    b, t, _ = ml.shape
    chunk = min(chunk, t)
    assert t % chunk == 0 and b % nb == 0
    nc = t // chunk
    ifr = jnp.transpose(ifc[:, :, :2 * ML_HEADS].reshape(b, nc, chunk, 2 * ML_HEADS), (0, 1, 3, 2))
    kern = functools.partial(_mlstm_kernel, chunk=chunk)

    def colblock(cb):
        return pl.BlockSpec((nb, chunk, BRANCH_W), lambda bi, ci: (bi, ci, cb))

    def state(shape):
        return pl.BlockSpec((nb,) + shape, lambda bi, ci: (bi,) + (0,) * len(shape))

    c_shape = (ML_HEADS, ML_HEAD_DIM, ML_HEAD_DIM)
    n_shape = (ML_HEADS, 1, ML_HEAD_DIM)
    m_shape = (ML_HEADS, 1, LANES)
    return pl.pallas_call(
        kern,
        grid=(b // nb, nc),
        in_specs=[colblock(0), colblock(1), colblock(2), colblock(3), colblock(2),
                  pl.BlockSpec((nb, chunk, LANES), lambda bi, ci: (bi, ci, 0)),
                  pl.BlockSpec((nb, 1, 2 * ML_HEADS, chunk), lambda bi, ci: (bi, ci, 0, 0)),
                  pl.BlockSpec((None, 1, BRANCH_W), lambda bi, ci: (layer, 0, 0)),
                  state(c_shape), state(n_shape), state(m_shape)],
        out_specs=[pl.BlockSpec((nb, chunk, BRANCH_W), lambda bi, ci: (bi, ci, 0)),
                   state(c_shape), state(n_shape), state(m_shape)],
        out_shape=[jax.ShapeDtypeStruct((b, t, BRANCH_W), BF16),
                   jax.ShapeDtypeStruct((b,) + c_shape, F32),
                   jax.ShapeDtypeStruct((b,) + n_shape, F32),
                   jax.ShapeDtypeStruct((b,) + m_shape, F32)],
        compiler_params=_cparams("parallel", "arbitrary"),
        name="mlstm",
    )(ml, ml, ml, ml, zg, ifc, ifr, hnorm_all, c0, n0, m0)


def _merge_kernel(y0_ref, y1_ref, y2_ref, g0_ref, g1_ref, g2_ref, w_ref, o_ref):
    acc = None
    for bi, (y_ref, g_ref) in enumerate(((y0_ref, g0_ref), (y1_ref, g1_ref), (y2_ref, g2_ref))):
        term = _sigmoid(g_ref[...].astype(F32)) * jnp.dot(y_ref[...], w_ref[bi], preferred_element_type=F32)
        acc = term if acc is None else acc + term
    o_ref[...] = acc.astype(o_ref.dtype)


def _merge(ys, zg2d, w_branch_all, layer, tm=1024, tn=1024):
    m = zg2d.shape[0]
    tm = min(tm, m)
    assert m % tm == 0 and D_MODEL % tn == 0 and ZG_GATE0 % tn == 0
    gb = ZG_GATE0 // tn

    def gate(bi):
        return pl.BlockSpec((tm, tn), lambda i, j: (i, gb + bi * (D_MODEL // tn) + j))

    y_spec = pl.BlockSpec((tm, BRANCH_W), lambda i, j: (i, 0))
    return pl.pallas_call(
        _merge_kernel,
        grid=(m // tm, D_MODEL // tn),
        in_specs=[y_spec, y_spec, y_spec, gate(0), gate(1), gate(2),
                  pl.BlockSpec((None, N_BRANCH, BRANCH_W, tn), lambda i, j: (layer, 0, 0, j))],
        out_specs=pl.BlockSpec((tm, tn), lambda i, j: (i, j)),
        out_shape=jax.ShapeDtypeStruct((m, D_MODEL), BF16),
        compiler_params=_cparams("parallel", "arbitrary"),
        name="merge",
    )(*ys, zg2d, zg2d, zg2d, w_branch_all)


def _out_kernel(a_ref, w_ref, x_ref, gpost_ref, gnext_ref, xo_ref, xn_ref):
    y = jnp.dot(a_ref[...], w_ref[...], preferred_element_type=F32)
    yn = y * lax.rsqrt(jnp.mean(y * y, axis=-1, keepdims=True) + RMS_EPS) * gpost_ref[...]
    xo = x_ref[...] + yn
    xo_ref[...] = xo
    xn = xo * lax.rsqrt(jnp.mean(xo * xo, axis=-1, keepdims=True) + RMS_EPS) * gnext_ref[...]
    xn_ref[...] = xn.astype(xn_ref.dtype)


def _out_proj(merged, w_out_all, x, g_post_all, g_pre_all, layer, tm=512):
    m = x.shape[0]
    tm = min(tm, m)
    assert m % tm == 0
    nxt = (layer + 1) % g_pre_all.shape[0]
    row = pl.BlockSpec((tm, D_MODEL), lambda i: (i, 0))
    return pl.pallas_call(
        _out_kernel,
        grid=(m // tm,),
        in_specs=[row,
                  pl.BlockSpec((None, D_MODEL, D_MODEL), lambda i: (layer, 0, 0)),
                  row,
                  pl.BlockSpec((None, 1, D_MODEL), lambda i: (layer, 0, 0)),
                  pl.BlockSpec((None, 1, D_MODEL), lambda i: (nxt, 0, 0))],
        out_specs=[row, row],
        out_shape=[jax.ShapeDtypeStruct((m, D_MODEL), F32), jax.ShapeDtypeStruct((m, D_MODEL), BF16)],
        compiler_params=_cparams("parallel"),
        name="out_proj",
    )(merged, w_out_all, x, g_post_all, g_pre_all)


def _trunk_layer(x, xn, shape, pos0, pool_prev16, kv_past, kv_new, c0, n0, m0, params, layer,
                 sb_tk=256, sb_tq=256, sb_heads=8, ml_chunk=256):
    b, t = shape
    wa, wz, w_if, b_if_pad, w_pool, pool_scale, hnorm, w_branch, w_out, g_post, g_pre = params
    u, q16, ifc = _pool_if_proj(xn, wa, w_if, b_if_pad, layer)
    k_all, k16 = _mm_heads(xn, wa, layer, OFF_SB_K, kv_new[0])
    v_all, v16 = _mm_heads(xn, wa, layer, OFF_SB_V, kv_new[1])
    (ml,) = _mm(xn, wa, layer, OFF_ML_Q, ML_COLS, (BF16,))
    (zg,) = _mm(xn, wz, layer, 0, ZG_COLS, (BF16,))

    u3 = u.reshape(b, t, BRANCH_W)
    zg3 = zg.reshape(b, t, ZG_COLS)
    y_pool = _pool(u3, pool_prev16, zg3, w_pool, pool_scale, layer, pos0)

    past = None if kv_past is None else kv_past + (layer,)
    y_sb = _stick_breaking(q16.reshape(b, t, BRANCH_W), zg3, k16.reshape(b, t, BRANCH_W),
                           v16.reshape(b, t, BRANCH_W), past, sb_tq, sb_tk, sb_heads)

    y_ml, c_new, n_new, m_new = _mlstm(ml.reshape(b, t, ML_COLS), zg3, ifc.reshape(b, t, LANES), hnorm, layer,
                                       c0, n0, m0, ml_chunk)

    ys = [y.reshape(b * t, BRANCH_W) for y in (y_pool, y_sb, y_ml)]
    merged = _merge(ys, zg, w_branch, layer)
    x_out, xn_next = _out_proj(merged, w_out, x, g_post, g_pre, layer)

    pool_new = u3[:, t - POOL_STATE:, :]
    return x_out, xn_next, (k_all, v_all), (pool_new, c_new, n_new[:, :, 0, :], m_new[:, :, 0, 0])


def _cast_kernel(x_ref, o_ref):
    o_ref[...] = x_ref[...].astype(o_ref.dtype)


def _scaled_cast_kernel(x_ref, s_ref, o_ref):
    o_ref[...] = (x_ref[...] * s_ref[...]).astype(o_ref.dtype)


def _cast_bf16(x, col0, ncols, col_scale=None, tr=512, tc=2048):
    d, r, _ = x.shape
    tr = min(tr, r)
    assert r % tr == 0 and ncols % tc == 0 and col0 % tc == 0
    cb = col0 // tc
    blk = pl.BlockSpec((None, tr, tc), lambda i, j, k: (i, j, k))
    in_specs, args, kern = [pl.BlockSpec((None, tr, tc), lambda i, j, k: (i, j, cb + k))], [x], _cast_kernel
    if col_scale is not None:
        in_specs.append(pl.BlockSpec((1, tc), lambda i, j, k: (0, k)))
        args.append(col_scale)
        kern = _scaled_cast_kernel
    return pl.pallas_call(
        kern,
        grid=(d, r // tr, ncols // tc),
        in_specs=in_specs,
        out_specs=blk,
        out_shape=jax.ShapeDtypeStruct((d, r, ncols), BF16),
        compiler_params=_cparams("parallel", "parallel", "parallel"),
        name="cast_bf16",
    )(*args)


def _shifted_cast_kernel(x_ref, nxt_ref, o_ref, *, shift):
    x = jnp.concatenate([x_ref[:, shift:], nxt_ref[:, :shift]], axis=1)
    o_ref[...] = x.astype(o_ref.dtype)


def _shifted_cast_bf16(x, col0, shift, ncols, tr=512, tc=1024):
    d, r, c = x.shape
    assert r % tr == 0 and ncols % tc == 0 and col0 % tc == 0 and 0 < shift < LANES
    assert col0 + shift + ncols <= c
    cb, nb = col0 // tc, tc // LANES
    return pl.pallas_call(
        functools.partial(_shifted_cast_kernel, shift=shift),
        grid=(d, r // tr, ncols // tc),
        in_specs=[pl.BlockSpec((None, tr, tc), lambda i, j, k: (i, j, cb + k)),
                  pl.BlockSpec((None, tr, LANES), lambda i, j, k: (i, j, (cb + k + 1) * nb))],
        out_specs=pl.BlockSpec((None, tr, tc), lambda i, j, k: (i, j, k)),
        out_shape=jax.ShapeDtypeStruct((d, r, ncols), BF16),
        compiler_params=_cparams("parallel", "parallel", "parallel"),
        name="shifted_cast_bf16",
    )(x, x)


def _prep_params(g_pre, w_in, b_if, w_pool, pool_scale, ml_hnorm, w_branch, w_out, g_post):
    depth = w_in.shape[0]
    q_scale = SB_HEAD_DIM ** -0.5 * LOG2_E
    cols = lax.broadcasted_iota(jnp.int32, (1, OFF_ML_I), 1)
    col_scale = jnp.where((cols >= OFF_SB_Q) & (cols < OFF_SB_K), q_scale, 1.0).astype(F32)
    wa = _cast_bf16(w_in, 0, OFF_ML_I, col_scale)
    wz = _shifted_cast_bf16(w_in, OFF_ML_I, OFF_Z - OFF_ML_I, ZG_COLS)
    gate_cols = (lax.broadcasted_iota(jnp.int32, (1, LANES), 1) < 2 * ML_HEADS).astype(F32)
    w_if = _cast_bf16(w_in, OFF_ML_I, LANES, gate_cols, tc=LANES)
    b_if_pad = jnp.pad(b_if.reshape(depth, 1, 2 * ML_HEADS), ((0, 0), (0, 0), (0, LANES - 2 * ML_HEADS)))
    w_branch16 = _cast_bf16(w_branch.reshape(depth * N_BRANCH, BRANCH_W, D_MODEL), 0, D_MODEL)
    w_out16 = _cast_bf16(w_out, 0, D_MODEL)
    return (wa, wz, w_if, b_if_pad, w_pool.astype(BF16), pool_scale.reshape(depth, 1, BRANCH_W),
            ml_hnorm.reshape(depth, 1, BRANCH_W), w_branch16.reshape(depth, N_BRANCH, BRANCH_W, D_MODEL), w_out16,
            g_post.reshape(depth, 1, D_MODEL), g_pre.reshape(depth, 1, D_MODEL))


def kernel(x_prompt, x_sample, cache_sb_k, cache_sb_v, state_pool, state_ml_c, state_ml_n, state_ml_m,
           g_pre, w_in, b_if, w_pool, pool_scale, ml_hnorm, w_branch, w_out, g_post):
    bp, tp, _ = x_prompt.shape
    bs, ts, _ = x_sample.shape
    past_len = cache_sb_k.shape[2]
    params = _prep_params(g_pre, w_in, b_if, w_pool, pool_scale, ml_hnorm, w_branch, w_out, g_post)
    g_pre3 = params[-1]

    xp = x_prompt.reshape(bp * tp, D_MODEL)
    xs = x_sample.reshape(bs * ts, D_MODEL)
    xnp = _rmsnorm(xp, g_pre3, 0)
    xns = _rmsnorm(xs, g_pre3, 0)

    zero_pool = jnp.zeros((bp, POOL_HALO, BRANCH_W), F32)
    zero_c = jnp.zeros((bp, ML_HEADS, ML_HEAD_DIM, ML_HEAD_DIM), F32)
    zero_n = jnp.zeros((bp, ML_HEADS, 1, ML_HEAD_DIM), F32)
    zero_m = jnp.zeros((bp, ML_HEADS, 1, LANES), F32)
    pool_prev = jnp.pad(state_pool, ((0, 0), (0, 0), (POOL_HALO - POOL_STATE, 0), (0, 0)))
    n_prev = state_ml_n[:, :, :, None, :]
    m_prev = jnp.broadcast_to(state_ml_m[:, :, :, None, None], state_ml_m.shape + (1, LANES))

    kv_past = (cache_sb_k.reshape(DEPTH, bs, past_len * SB_HEADS, SB_HEAD_DIM),
               cache_sb_v.reshape(DEPTH, bs, past_len * SB_HEADS, SB_HEAD_DIM))

    kv_p = tuple(jnp.zeros((DEPTH, bp * tp * SB_HEADS, SB_HEAD_DIM), F32) for _ in range(2))
    kv_s = tuple(jnp.zeros((DEPTH, bs * ts * SB_HEADS, SB_HEAD_DIM), F32) for _ in range(2))

    pr, sa = [], []
    for l in range(DEPTH):
        xp, xnp, kv_p, outs_p = _trunk_layer(xp, xnp, (bp, tp), 0, zero_pool, None, kv_p,
                                             zero_c, zero_n, zero_m, params, l)
        xs, xns, kv_s, outs_s = _trunk_layer(xs, xns, (bs, ts), past_len, pool_prev[l], kv_past, kv_s,
                                             state_ml_c[l], n_prev[l], m_prev[l], params, l, sb_tq=ts)
        pr.append(outs_p)
        sa.append(outs_s)
    p_pool, p_c, p_n, p_m = [jnp.stack(a, axis=0) for a in zip(*pr)]
    s_pool, s_c, s_n, s_m = [jnp.stack(a, axis=0) for a in zip(*sa)]
    p_k, p_v = [a.reshape(DEPTH, bp, tp, SB_HEADS, SB_HEAD_DIM) for a in kv_p]
    s_k, s_v = [a.reshape(DEPTH, bs, ts, SB_HEADS, SB_HEAD_DIM) for a in kv_s]
    return (xp.reshape(bp, tp, D_MODEL), xs.reshape(bs, ts, D_MODEL),
            p_k, p_v, p_pool, p_c, p_n, p_m, s_k, s_v, s_pool, s_c, s_n, s_m)
```
